```python
import math
import jax
import jax.numpy as jnp
from jax import lax
import numpy as np

D_MODEL = 1024
BATCH = 8
SEQ = 2048
DEPTH = 4

N_MIXERS = 4
GROUP_WIDTH = D_MODEL // N_MIXERS
HEAD_DIM = 64
GROUP_HEADS = GROUP_WIDTH // HEAD_DIM
MIX_WIDTH = N_MIXERS * GROUP_WIDTH
DILATED_PATTERNS = ((128, 1), (512, 4), (2048, 16))
MLA_Q_RANK = 384
MLA_KV_RANK = 256
MLA_NOPE_DIM = 64
MLA_ROPE_DIM = 32
MLA_V_DIM = HEAD_DIM
DIFF_QK_DIM = HEAD_DIM // 2
DIFF_EPS = 1e-5
GRID_W = 64
NA_WIN_ROWS = 8
NA_WIN_COLS = 16
D_FF = 2816
CONV_WIDTH = 3
PLE_DIM = 256

ROPE_THETA = 10000.0
NORM_EPS = 1e-6
Q_BLOCK = 128
NEG_INF = -1e30

A_COLS = 3 * GROUP_WIDTH
MLA_COLS = MLA_Q_RANK + MLA_KV_RANK + MLA_ROPE_DIM
DIFF_COLS = 2 * (GROUP_HEADS * 2 * DIFF_QK_DIM) + GROUP_WIDTH
NA_COLS = 3 * GROUP_WIDTH
IN_COLS = A_COLS + MLA_COLS + DIFF_COLS + NA_COLS

kernel_name = 'hybrid_parallel_head_group_encoder'


def rms_norm(x, g, eps=NORM_EPS):
    xf = x.astype(jnp.float32)
    y = xf * lax.rsqrt(jnp.mean(xf * xf, axis=-1, keepdims=True) + eps)
    return (y * g.astype(jnp.float32)).astype(x.dtype)


def rope(x, pos):
    d = x.shape[-1]
    inv = jnp.power(ROPE_THETA, -jnp.arange(0, d, 2, dtype=jnp.float32) / d)
    ang = pos.astype(jnp.float32)[:, None] * inv[None, :]
    cos = jnp.cos(ang).astype(x.dtype)
    sin = jnp.sin(ang).astype(x.dtype)
    x1, x2 = jnp.split(x, 2, axis=-1)
    return jnp.concatenate([x1 * cos - x2 * sin, x1 * sin + x2 * cos], axis=-1)


def split_heads(t, n):
    b, s, _ = t.shape
    return t.reshape(b, s, n, -1).transpose(0, 2, 1, 3)


def merge_heads(t):
    b, h, s, d = t.shape
    return t.transpose(0, 2, 1, 3).reshape(b, s, h * d)


def softmax_f32(s):
    return jax.nn.softmax(s.astype(jnp.float32), axis=-1)


def block_sweep(f, *qs):
    b, h, s, _ = qs[0].shape
    nb = s // Q_BLOCK
    blocks = tuple(jnp.moveaxis(q.reshape(b, h, nb, Q_BLOCK, q.shape[-1]), 2, 0) for q in qs)
    out = lax.map(lambda args: f(*args), blocks)
    out = jnp.moveaxis(out, 0, 2)
    return out.reshape(b, h, s, out.shape[-1])


def banded_attn(q, k, v, radius):
    lead = q.shape[:-2]
    L, d = q.shape[-2:]
    nb = -(-L // radius)
    lp = nb * radius
    zeros = [(0, 0)] * len(lead)
    qb = jnp.pad(q, zeros + [(0, lp - L), (0, 0)]).reshape(*lead, nb, radius, d)

    def windows(t):
        tb = jnp.pad(t, zeros + [(radius, lp - L + radius), (0, 0)])
        tb = tb.reshape(*lead, nb + 2, radius, t.shape[-1])
        return jnp.concatenate([tb[..., :-2, :, :], tb[..., 1:-1, :, :], tb[..., 2:, :, :]], axis=-2)

    kw, vw = windows(k), windows(v)
    s = jnp.einsum('...nqd,...nkd->...nqk', qb, kw).astype(jnp.float32) * (d ** -0.5)
    blk = jnp.arange(nb)[:, None, None] * radius
    qpos = blk + jnp.arange(radius)[None, :, None]
    kpos = blk - radius + jnp.arange(3 * radius)[None, None, :]
    mask = (jnp.abs(kpos - qpos) <= radius) & (kpos >= 0) & (kpos < L)
    s = jnp.where(mask, s, NEG_INF)
    lse = jax.nn.logsumexp(s, axis=-1)
    prob = jnp.exp(s - lse[..., None])
    out = jnp.einsum('...nqk,...nkd->...nqd', prob.astype(v.dtype), vw)
    return out.reshape(*lead, lp, d)[..., :L, :], lse.reshape(*lead, lp)[..., :L]


def dilated_mixture(q, k, v):
    b, h, s, d = q.shape
    outs, lses = [], []
    for window, dil in DILATED_PATTERNS:
        radius = window // (2 * dil)

        def to_residue(t):
            return t.reshape(b, h, s // dil, dil, t.shape[-1]).swapaxes(2, 3)

        o, l = banded_attn(to_residue(q), to_residue(k), to_residue(v), radius)
        outs.append(o.swapaxes(2, 3).reshape(b, h, s, d))
        lses.append(l.swapaxes(2, 3).reshape(b, h, s))
    wts = jax.nn.softmax(jnp.stack(lses), axis=0)
    out = jnp.sum(wts[..., None] * jnp.stack(outs).astype(jnp.float32), axis=0)
    return out.astype(q.dtype)


def mla_mixer(cols, q_norm, w_uq, kv_norm, w_ukv, pos):
    c_q, c_kv, k_rope = jnp.split(cols, [MLA_Q_RANK, MLA_Q_RANK + MLA_KV_RANK], axis=-1)
    q = split_heads(rms_norm(c_q, q_norm) @ w_uq, GROUP_HEADS)
    q_nope, q_rot = jnp.split(q, [MLA_NOPE_DIM], axis=-1)
    q = jnp.concatenate([q_nope, rope(q_rot, pos)], axis=-1)
    kv = split_heads(rms_norm(c_kv, kv_norm) @ w_ukv, GROUP_HEADS)
    k_nope, v = jnp.split(kv, [MLA_NOPE_DIM], axis=-1)
    k_r = rope(k_rope[:, None], pos)
    k = jnp.concatenate([k_nope, jnp.broadcast_to(k_r, k_nope.shape[:-1] + (MLA_ROPE_DIM,))], axis=-1)
    scale = (MLA_NOPE_DIM + MLA_ROPE_DIM) ** -0.5

    def attend(qb):
        pr = softmax_f32(jnp.einsum('bhqd,bhkd->bhqk', qb, k).astype(jnp.float32) * scale)
        return jnp.einsum('bhqk,bhkd->bhqd', pr.astype(v.dtype), v)

    return block_sweep(attend, q)


def diff_mixer(cols, lam_q1, lam_k1, lam_q2, lam_k2, subln, lam_init, pos):
    qk_w = GROUP_HEADS * 2 * DIFF_QK_DIM
    q, k, v = jnp.split(cols, [qk_w, 2 * qk_w], axis=-1)
    b, s, _ = q.shape
    q = rope(split_heads(q, 2 * GROUP_HEADS), pos).reshape(b, GROUP_HEADS, 2, s, DIFF_QK_DIM)
    k = rope(split_heads(k, 2 * GROUP_HEADS), pos).reshape(b, GROUP_HEADS, 2, s, DIFF_QK_DIM)
    v = split_heads(v, GROUP_HEADS)
    q1, q2 = q[:, :, 0], q[:, :, 1]
    k1, k2 = k[:, :, 0], k[:, :, 1]
    lam = (jnp.exp(jnp.sum(lam_q1.astype(jnp.float32) * lam_k1.astype(jnp.float32)))
           - jnp.exp(jnp.sum(lam_q2.astype(jnp.float32) * lam_k2.astype(jnp.float32))) + lam_init)
    scale = DIFF_QK_DIM ** -0.5

    def attend(q1b, q2b):
        p1 = softmax_f32(jnp.einsum('bhqd,bhkd->bhqk', q1b, k1).astype(jnp.float32) * scale)
        p2 = softmax_f32(jnp.einsum('bhqd,bhkd->bhqk', q2b, k2).astype(jnp.float32) * scale)
        return jnp.einsum('bhqk,bhkd->bhqd', (p1 - lam * p2).astype(v.dtype), v)

    o = block_sweep(attend, q1, q2)
    return rms_norm(o, subln, DIFF_EPS) * (1.0 - lam_init)


def neighbourhood_attn(q, k, v, rpb):
    b, h, s, d = q.shape
    rows = s // GRID_W
    kr = min(NA_WIN_ROWS, rows)
    q = q.reshape(b, h, rows, GRID_W, d)
    k = k.reshape(b, h, rows, GRID_W, d)
    v = v.reshape(b, h, rows, GRID_W, d)
    r = jnp.arange(rows)
    key_rows = jnp.clip(r - kr // 2, 0, rows - kr)[:, None] + jnp.arange(kr)[None, :]
    kg = k[:, :, key_rows]
    vg = v[:, :, key_rows]
    sc = jnp.einsum('bhrqd,bhrjkd->bhrqjk', q, kg).astype(jnp.float32) * (d ** -0.5)
    c = jnp.arange(GRID_W)
    c_start = jnp.clip(c - NA_WIN_COLS // 2, 0, GRID_W - NA_WIN_COLS)
    col_ok = (c[None, :] >= c_start[:, None]) & (c[None, :] < c_start[:, None] + NA_WIN_COLS)
    dr = key_rows - r[:, None]
    dc = jnp.clip(c[None, :] - c[:, None], -(NA_WIN_COLS - 1), NA_WIN_COLS - 1)
    idx_r = dr[:, None, :, None] + (NA_WIN_ROWS - 1)
    idx_c = dc[None, :, None, :] + (NA_WIN_COLS - 1)
    bias = rpb[:, idx_r, idx_c].astype(jnp.float32)
    sc = jnp.where(col_ok[None, None, None, :, None, :], sc + bias[None], NEG_INF)
    pr = softmax_f32(sc.reshape(b, h, rows, GRID_W, kr * GRID_W)).reshape(sc.shape)
    o = jnp.einsum('bhrqjk,bhrjkd->bhrqd', pr.astype(v.dtype), vg)
    return o.reshape(b, h, s, d)


def dwconv_centred(u, w, bias):
    s = u.shape[1]
    half = CONV_WIDTH // 2
    up = jnp.pad(u, ((0, 0), (half, CONV_WIDTH - 1 - half), (0, 0)))
    return sum(up[:, j:j + s] * w[j] for j in range(CONV_WIDTH)) + bias


def setup_inputs(seed: int = 0) -> dict:
    key = jax.random.key(seed)
    ks = jax.random.split(key, 24)

    def nrm(k, shape, scale):
        return jax.random.normal(k, shape, jnp.float32) * scale

    def gain(k, shape):
        return 1.0 + nrm(k, shape, 0.01)

    return {
        'x': nrm(ks[0], (BATCH, SEQ, D_MODEL), 1.0),
        'p': nrm(ks[1], (DEPTH, BATCH, SEQ, PLE_DIM), 1.0),
        'attn_norm': gain(ks[2], (DEPTH, D_MODEL)),
        'w_in': nrm(ks[3], (DEPTH, D_MODEL, IN_COLS), D_MODEL ** -0.5),
        'mla_q_norm': gain(ks[4], (DEPTH, MLA_Q_RANK)),
        'w_uq': nrm(ks[5], (DEPTH, MLA_Q_RANK, GROUP_HEADS * (MLA_NOPE_DIM + MLA_ROPE_DIM)), MLA_Q_RANK ** -0.5),
        'mla_kv_norm': gain(ks[6], (DEPTH, MLA_KV_RANK)),
        'w_ukv': nrm(ks[7], (DEPTH, MLA_KV_RANK, GROUP_HEADS * (MLA_NOPE_DIM + MLA_V_DIM)), MLA_KV_RANK ** -0.5),
        'lam_q1': nrm(ks[8], (DEPTH, DIFF_QK_DIM), 0.1),
        'lam_k1': nrm(ks[9], (DEPTH, DIFF_QK_DIM), 0.1),
        'lam_q2': nrm(ks[10], (DEPTH, DIFF_QK_DIM), 0.1),
        'lam_k2': nrm(ks[11], (DEPTH, DIFF_QK_DIM), 0.1),
        'diff_subln': gain(ks[12], (DEPTH, HEAD_DIM)),
        'na_rpb': nrm(ks[13], (DEPTH, GROUP_HEADS, 2 * NA_WIN_ROWS - 1, 2 * NA_WIN_COLS - 1), 0.1),
        'w_o': nrm(ks[14], (DEPTH, MIX_WIDTH, D_MODEL), MIX_WIDTH ** -0.5),
        'ffn_norm': gain(ks[15], (DEPTH, D_MODEL)),
        'w_up': nrm(ks[16], (DEPTH, D_MODEL, 2 * D_FF), D_MODEL ** -0.5),
        'conv_w': nrm(ks[17], (DEPTH, CONV_WIDTH, 2 * D_FF), CONV_WIDTH ** -0.5),
        'conv_b': nrm(ks[18], (DEPTH, 2 * D_FF), 0.01),
        'w_down': nrm(ks[19], (DEPTH, D_FF, D_MODEL), D_FF ** -0.5),
        'ple_norm': gain(ks[20], (DEPTH, D_MODEL)),
        'w_ple_gate': nrm(ks[21], (DEPTH, D_MODEL, D_MODEL), D_MODEL ** -0.5),
        'w_ple_proj': nrm(ks[22], (DEPTH, PLE_DIM, D_MODEL), PLE_DIM ** -0.5),
        'final_norm': gain(ks[23], (D_MODEL,)),
    }


def reference(x, p, attn_norm, w_in, mla_q_norm, w_uq, mla_kv_norm, w_ukv, lam_q1, lam_k1, lam_q2, lam_k2,
              diff_subln, na_rpb, w_o, ffn_norm, w_up, conv_w, conv_b, w_down, ple_norm, w_ple_gate,
              w_ple_proj, final_norm):
    s = x.shape[1]
    pos = jnp.arange(s, dtype=jnp.int32)
    split_at = [A_COLS, A_COLS + MLA_COLS, A_COLS + MLA_COLS + DIFF_COLS]
    h = x
    for i in range(DEPTH):
        hn = rms_norm(h, attn_norm[i])
        cols = hn @ w_in[i]
        a_cols, b_cols, c_cols, d_cols = jnp.split(cols, split_at, axis=-1)
        qa, ka, va = (split_heads(t, GROUP_HEADS) for t in jnp.split(a_cols, 3, axis=-1))
        o_a = dilated_mixture(rope(qa, pos), rope(ka, pos), va)
        o_b = mla_mixer(b_cols, mla_q_norm[i], w_uq[i], mla_kv_norm[i], w_ukv[i], pos)
        lam_init = 0.8 - 0.6 * math.exp(-0.3 * i)
        o_c = diff_mixer(c_cols, lam_q1[i], lam_k1[i], lam_q2[i], lam_k2[i], diff_subln[i], lam_init, pos)
        qd, kd, vd = (split_heads(t, GROUP_HEADS) for t in jnp.split(d_cols, 3, axis=-1))
        o_d = neighbourhood_attn(qd, kd, vd, na_rpb[i])
        mix = jnp.concatenate([merge_heads(o) for o in (o_a, o_b, o_c, o_d)], axis=-1)
        h = h + mix @ w_o[i]
        hn = rms_norm(h, ffn_norm[i])
        u = dwconv_centred(hn @ w_up[i], conv_w[i], conv_b[i])
        gate, val = jnp.split(u, 2, axis=-1)
        h = h + (jax.nn.gelu(gate) * val) @ w_down[i]
        e = p[i] @ w_ple_proj[i]
        g = jax.nn.sigmoid(rms_norm(h, ple_norm[i]) @ w_ple_gate[i])
        h = h + g * e
    return rms_norm(h, final_norm)
```

```python
import functools
import math

import numpy as np
import jax
import jax.numpy as jnp
from jax import lax
from jax.experimental import pallas as pl
from jax.experimental.pallas import tpu as pltpu

D_MODEL = 1024
SEQ = 2048
DEPTH = 4
GROUP_WIDTH = 256
HEAD_DIM = 64
LANES = 128
DILATED_PATTERNS = ((128, 1), (512, 4), (2048, 16))
RADIUS = 64
MLA_Q_RANK = 384
MLA_KV_RANK = 256
MLA_NOPE_DIM = 64
MLA_ROPE_DIM = 32
DIFF_QK_DIM = 32
DIFF_EPS = 1e-5
GRID_W = 64
NA_WIN_ROWS = 8
NA_WIN_COLS = 16
D_FF = 2816
PLE_DIM = 256
ROPE_THETA = 10000.0
NORM_EPS = 1e-6
NEG_INF = -1e30

F32 = jnp.float32
BF16 = jnp.bfloat16

IN_A = 0
IN_CQ = 768
IN_CKV = 1152
IN_KR = 1408
IN_C = 1536
IN_D = 2304
IN_COLS_P = 3072

VMEM_LIMIT = 56 * 1024 * 1024

IN_TM = 512
ATT_TQ = 256
A_TQ = 128
NA_ROWS = 4
NA_KROWS = 12
O_TM = 1024
FF_TS = 1024
FF_HALO = 16
FF_CHUNK = 256


def _dot(a, b):
    return jnp.dot(a, b, preferred_element_type=F32)


def _dot_nt(a, b):
    return lax.dot_general(a, b, (((1,), (1,)), ((), ())), preferred_element_type=F32)


def _rms(x, gain, eps):
    ms = jnp.mean(x * x, axis=-1, keepdims=True)
    return x * lax.rsqrt(ms + eps) * gain


def _params(sem):
    return pltpu.CompilerParams(dimension_semantics=sem, vmem_limit_bytes=VMEM_LIMIT)


def _rope_tables(d, groups, scale):
    half = d // 2
    pos = jnp.arange(SEQ, dtype=jnp.int32)
    inv = jnp.power(ROPE_THETA, -jnp.arange(0, d, 2, dtype=F32) / d)
    ang = pos.astype(F32)[:, None] * inv[None, :]
    cos, sin = jnp.cos(ang), jnp.sin(ang)
    idx = np.zeros(LANES, np.int32)
    first = np.zeros(LANES, bool)
    second = np.zeros(LANES, bool)
    for g in groups:
        for j in range(d):
            idx[g + j] = j % half
            (first if j < half else second)[g + j] = True
    is_rope = first | second
    c = jnp.where(is_rope[None, :], cos[:, idx], 1.0) * scale
    sa = jnp.where(first[None, :], -sin[:, idx], 0.0) * scale
    sb = jnp.where(second[None, :], sin[:, idx], 0.0) * scale
    return jnp.stack([c, sa, sb]).astype(F32)


def _rope(x, tab_ref, half):
    return (x * tab_ref[0] + pltpu.roll(x, LANES - half, 1) * tab_ref[1]
            + pltpu.roll(x, half, 1) * tab_ref[2])


def _inproj_kernel(x_ref, g_ref, win_ref, qn_ref, wuq_ref, kvn_ref, wukv_ref,
                   ta_ref, tbq_ref, tbk_ref, tcq_ref, tck_ref,
                   aq_ref, ak_ref, av_ref, bq_ref, bk_ref, bv_ref,
                   cq_ref, ck_ref, cv_ref, dq_ref, dk_ref, dv_ref):
    hn = _rms(x_ref[...], g_ref[...], NORM_EPS).astype(BF16)

    aa = _dot(hn, win_ref[:, IN_A:IN_A + 768])
    for s in range(2):
        lo = s * LANES
        aq_ref[:, lo:lo + LANES] = _rope(aa[:, lo:lo + LANES], ta_ref, 32).astype(BF16)
        ak_ref[:, lo:lo + LANES] = _rope(aa[:, 256 + lo:256 + lo + LANES], ta_ref, 32).astype(BF16)
    av_ref[...] = aa[:, 512:768].astype(BF16)

    cq = _dot(hn, win_ref[:, IN_CQ:IN_CQ + MLA_Q_RANK])
    qb = _dot(_rms(cq, qn_ref[...], NORM_EPS).astype(BF16), wuq_ref[...])
    ckv = _dot(hn, win_ref[:, IN_CKV:IN_CKV + MLA_KV_RANK])
    kvb = _dot(_rms(ckv, kvn_ref[...], NORM_EPS).astype(BF16), wukv_ref[...])
    kr = _rope(_dot(hn, win_ref[:, IN_KR:IN_KR + LANES]), tbk_ref, 16)
    for h in range(4):
        lo = h * LANES
        bq_ref[h] = _rope(qb[:, lo:lo + LANES], tbq_ref, 16).astype(BF16)
        bk_ref[h] = (kvb[:, lo:lo + LANES] + kr).astype(BF16)
    for s in range(2):
        bv_ref[s] = kvb[:, 512 + s * LANES:512 + (s + 1) * LANES].astype(BF16)

    cc = _dot(hn, win_ref[:, IN_C:IN_C + 768])
    for s in range(2):
        lo = s * LANES
        cq_ref[s] = _rope(cc[:, lo:lo + LANES], tcq_ref, 16).astype(BF16)
        ck_ref[s] = _rope(cc[:, 256 + lo:256 + lo + LANES], tck_ref, 16).astype(BF16)
        cv_ref[s] = cc[:, 512 + lo:512 + lo + LANES].astype(BF16)

    dd = _dot(hn, win_ref[:, IN_D:IN_D + 768])
    for s in range(2):
        lo = s * LANES
        dq_ref[s] = dd[:, lo:lo + LANES].astype(BF16)
        dk_ref[s] = dd[:, 256 + lo:256 + lo + LANES].astype(BF16)
        dv_ref[s] = dd[:, 512 + lo:512 + lo + LANES].astype(BF16)


def _inproj(h, g, win, qn, wuq, kvn, wukv, tabs):
    b = h.shape[0]
    tm = IN_TM
    grid = (b, SEQ // tm)
    full = lambda shape: pl.BlockSpec(shape, lambda i, j: (0,) * len(shape))
    tab = pl.BlockSpec((3, tm, LANES), lambda i, j: (0, j, 0))
    nat = pl.BlockSpec((None, tm, 256), lambda i, j: (i, j, 0))
    slab = lambda n: pl.BlockSpec((None, n, tm, LANES), lambda i, j: (i, 0, j, 0))
    nat_s = jax.ShapeDtypeStruct((b, SEQ, 256), BF16)
    slab_s = lambda n: jax.ShapeDtypeStruct((b, n, SEQ, LANES), BF16)
    return pl.pallas_call(
        _inproj_kernel,
        grid=grid,
        in_specs=[pl.BlockSpec((None, tm, D_MODEL), lambda i, j: (i, j, 0)),
                  full((1, D_MODEL)), full((D_MODEL, IN_COLS_P)),
                  full((1, MLA_Q_RANK)), full((MLA_Q_RANK, 512)),
                  full((1, MLA_KV_RANK)), full((MLA_KV_RANK, 768)),
                  tab, tab, tab, tab, tab],
        out_specs=[nat, nat, nat, slab(4), slab(4), slab(2),
                   slab(2), slab(2), slab(2), slab(2), slab(2), slab(2)],
        out_shape=[nat_s, nat_s, nat_s, slab_s(4), slab_s(4), slab_s(2),
                   slab_s(2), slab_s(2), slab_s(2), slab_s(2), slab_s(2), slab_s(2)],
        compiler_params=_params(("parallel", "parallel")),
        name="inproj",
    )(h, g, win, qn, wuq, kvn, wukv, *tabs)


def _band_kernel(q_ref, k_ref, v_ref, o_ref, l_ref, *, length, dil):
    tq = min(A_TQ, length)
    tk = min(tq + 2 * RADIUS, length)
    nblk = length // tq
    lane = lax.broadcasted_iota(jnp.int32, (tq, LANES), 1)
    head0 = lane < HEAD_DIM
    rel = (lax.broadcasted_iota(jnp.int32, (tq, tk), 1)
           - lax.broadcasted_iota(jnp.int32, (tq, tk), 0))

    def block(i, carry):
        q0 = pl.multiple_of(i * tq, tq)
        ws = pl.multiple_of(jnp.clip(q0 - RADIUS, 0, length - tk), RADIUS)
        d = rel + (ws - q0)
        band = jnp.abs(d) <= RADIUS
        for r in range(dil):
            for s in range(2):
                lo = r * 256 + s * LANES
                q = q_ref[pl.ds(q0, tq), lo:lo + LANES]
                kw = k_ref[pl.ds(ws, tk), lo:lo + LANES]
                vw = v_ref[pl.ds(ws, tk), lo:lo + LANES]
                outs, lses = [], []
                for t in range(2):
                    qm = jnp.where(head0 if t == 0 else ~head0, q, jnp.zeros_like(q))
                    sc = jnp.where(band, _dot_nt(qm, kw), NEG_INF)
                    m = jnp.max(sc, axis=-1, keepdims=True)
                    p = jnp.exp(sc - m)
                    den = jnp.sum(p, axis=-1, keepdims=True)
                    outs.append(_dot(p.astype(BF16), vw) / den)
                    lses.append(m + jnp.log(den))
                o_ref[pl.ds(q0, tq), lo:lo + LANES] = jnp.where(head0, outs[0], outs[1]).astype(BF16)
                l_ref[pl.ds(q0, tq), lo:lo + LANES] = jnp.where(head0, lses[0], lses[1])
        return carry

    lax.fori_loop(0, nblk, block, 0)


def _band_attention(q, k, v, dil):
    b = q.shape[0]
    length = SEQ // dil
    width = dil * 256
    view = lambda t: t.reshape(b, length, width)
    spec = pl.BlockSpec((None, length, width), lambda i: (i, 0, 0))
    o, lse = pl.pallas_call(
        functools.partial(_band_kernel, length=length, dil=dil),
        grid=(b,),
        in_specs=[spec, spec, spec],
        out_specs=[spec, spec],
        out_shape=[jax.ShapeDtypeStruct((b, length, width), BF16),
                   jax.ShapeDtypeStruct((b, length, width), F32)],
        compiler_params=_params(("parallel",)),
        name=f"band_dil{dil}",
    )(view(q), view(k), view(v))
    return o.reshape(b, SEQ, 256), lse.reshape(b, SEQ, 256)


def _mla_kernel(q_ref, k_ref, v_ref, o_ref):
    v = v_ref[...]
    outs = []
    for t in range(2):
        sc = _dot_nt(q_ref[t], k_ref[t])
        m = jnp.max(sc, axis=-1, keepdims=True)
        p = jnp.exp(sc - m)
        den = jnp.sum(p, axis=-1, keepdims=True)
        outs.append(_dot(p.astype(BF16), v) / den)
    lane = lax.broadcasted_iota(jnp.int32, outs[0].shape, 1)
    o_ref[...] = jnp.where(lane < HEAD_DIM, outs[0], outs[1]).astype(BF16)


def _mla_attention(q, k, v):
    b = q.shape[0]
    tq = ATT_TQ
    return pl.pallas_call(
        _mla_kernel,
        grid=(b, 2, SEQ // tq),
        in_specs=[pl.BlockSpec((None, 2, tq, LANES), lambda i, s, j: (i, s, j, 0)),
                  pl.BlockSpec((None, 2, SEQ, LANES), lambda i, s, j: (i, s, 0, 0)),
                  pl.BlockSpec((None, None, SEQ, LANES), lambda i, s, j: (i, s, 0, 0))],
        out_specs=pl.BlockSpec((None, None, tq, LANES), lambda i, s, j: (i, s, j, 0)),
        out_shape=jax.ShapeDtypeStruct((b, 2, SEQ, LANES), BF16),
        compiler_params=_params(("parallel", "parallel", "parallel")),
        name="mla_attn",
    )(q, k, v)


def _diff_kernel(lam_ref, sub_ref, q_ref, k_ref, v_ref, o_ref, *, lam_init):
    lv = lam_ref[...]
    lam = (jnp.exp(jnp.sum(lv[0:1] * lv[1:2], axis=-1, keepdims=True))
           - jnp.exp(jnp.sum(lv[2:3] * lv[3:4], axis=-1, keepdims=True)) + lam_init)
    q = q_ref[...]
    k = k_ref[...]
    v = v_ref[...]
    lane = lax.broadcasted_iota(jnp.int32, q.shape, 1)
    outs = []
    for t in range(2):
        probs, dens = [], []
        for u in range(2):
            lo = t * HEAD_DIM + u * DIFF_QK_DIM
            qm = jnp.where((lane >= lo) & (lane < lo + DIFF_QK_DIM), q, jnp.zeros_like(q))
            sc = _dot_nt(qm, k)
            m = jnp.max(sc, axis=-1, keepdims=True)
            p = jnp.exp(sc - m)
            probs.append(p)
            dens.append(jnp.sum(p, axis=-1, keepdims=True))
        pn = probs[0] * (1.0 / dens[0]) - probs[1] * (lam / dens[1])
        outs.append(_dot(pn.astype(BF16), v))
    head0 = lane < HEAD_DIM
    o = jnp.where(head0, outs[0], outs[1])
    o2 = o * o
    ss0 = jnp.sum(jnp.where(head0, o2, 0.0), axis=-1, keepdims=True)
    ss1 = jnp.sum(jnp.where(head0, 0.0, o2), axis=-1, keepdims=True)
    ms = jnp.where(head0, ss0, ss1) * (1.0 / HEAD_DIM)
    y = o * lax.rsqrt(ms + DIFF_EPS) * sub_ref[...]
    o_ref[...] = (y * (1.0 - lam_init)).astype(BF16)


def _diff_attention(lam_vecs, subln2, q, k, v, lam_init):
    b = q.shape[0]
    tq = ATT_TQ
    kv = pl.BlockSpec((None, None, SEQ, LANES), lambda i, s, j: (i, s, 0, 0))
    qo = pl.BlockSpec((None, None, tq, LANES), lambda i, s, j: (i, s, j, 0))
    return pl.pallas_call(
        functools.partial(_diff_kernel, lam_init=lam_init),
        grid=(b, 2, SEQ // tq),
        in_specs=[pl.BlockSpec((4, DIFF_QK_DIM), lambda i, s, j: (0, 0)),
                  pl.BlockSpec((1, LANES), lambda i, s, j: (0, 0)),
                  qo, kv, kv],
        out_specs=qo,
        out_shape=jax.ShapeDtypeStruct((b, 2, SEQ, LANES), BF16),
        compiler_params=_params(("parallel", "parallel", "parallel")),
        name="diff_attn",
    )(lam_vecs, subln2, q, k, v)


def _na_bias_kernel(rpb_ref, o_ref):
    nr = 2 * NA_WIN_ROWS - 1
    nc = 2 * NA_WIN_COLS - 1
    qc = lax.broadcasted_iota(jnp.int32, (GRID_W, LANES), 0)
    lane = lax.broadcasted_iota(jnp.int32, (GRID_W, LANES), 1)
    kc = lane % GRID_W
    left = lane < GRID_W
    start = jnp.clip(qc - NA_WIN_COLS // 2, 0, GRID_W - NA_WIN_COLS)
    col_ok = (kc >= start) & (kc < start + NA_WIN_COLS)
    idx_c = jnp.clip(kc - qc, -(NA_WIN_COLS - 1), NA_WIN_COLS - 1) + (NA_WIN_COLS - 1)
    sel = [idx_c == t for t in range(nc)]

    def body(i, carry):
        h = i // 16
        d = i % 16 - 8
        acc = jnp.zeros((GRID_W, LANES), F32)
        r_lo = jnp.clip(d + 7, 0, nr - 1)
        r_hi = jnp.clip(d + 8, 0, nr - 1)
        for t in range(nc):
            v_lo = rpb_ref[(h * nr + r_lo) * nc + t]
            v_hi = rpb_ref[(h * nr + r_hi) * nc + t]
            acc = acc + jnp.where(sel[t], jnp.where(left, v_lo, v_hi), 0.0)
        dr = jnp.where(left, d, d + 1)
        ok = col_ok & (dr >= -(NA_WIN_ROWS - 1)) & (dr <= NA_WIN_ROWS - 1)
        o_ref[i] = jnp.where(ok, acc, NEG_INF)
        return carry

    lax.fori_loop(0, 4 * 16, body, 0)


def _na_bias(rpb):
    return pl.pallas_call(
        _na_bias_kernel,
        in_specs=[pl.BlockSpec(memory_space=pltpu.SMEM)],
        out_specs=pl.BlockSpec(memory_space=pltpu.VMEM),
        out_shape=jax.ShapeDtypeStruct((4 * 16, GRID_W, LANES), F32),
        name="na_bias",
    )(rpb.reshape(-1))


def _na_kernel(bias_ref, q_ref, k_ref, v_ref, o_ref, *, pair):
    rows = SEQ // GRID_W
    tq = NA_ROWS * GRID_W
    tk = NA_KROWS * GRID_W
    lane = lax.broadcasted_iota(jnp.int32, (tq, LANES), 1)
    head0 = lane < HEAD_DIM
    left = lax.broadcasted_iota(jnp.int32, (GRID_W, LANES), 1) < GRID_W
    neg = jnp.full((GRID_W, LANES), NEG_INF, F32)
    for blk in range(rows // NA_ROWS):
        r0 = blk * NA_ROWS
        ws = min(max(r0 - NA_WIN_ROWS // 2, 0), rows - NA_KROWS)
        q = q_ref[r0 * GRID_W:r0 * GRID_W + tq, :]
        kw = k_ref[ws * GRID_W:ws * GRID_W + tk, :]
        vw = v_ref[ws * GRID_W:ws * GRID_W + tk, :]
        outs = []
        for t in range(2):
            h = 2 * pair + t
            rows_bias = []
            for ri in range(NA_ROWS):
                r = r0 + ri
                lo_r = min(max(r - NA_WIN_ROWS // 2, 0), rows - NA_WIN_ROWS) - r
                pieces = []
                for j2 in range(NA_KROWS // 2):
                    d = ws + 2 * j2 - r
                    ok_l = lo_r <= d <= lo_r + NA_WIN_ROWS - 1
                    ok_r = lo_r <= d + 1 <= lo_r + NA_WIN_ROWS - 1
                    if not (ok_l or ok_r):
                        pieces.append(neg)
                        continue
                    tab = bias_ref[h * 16 + d + 8]
                    if not ok_l:
                        tab = jnp.where(left, NEG_INF, tab)
                    elif not ok_r:
                        tab = jnp.where(left, tab, NEG_INF)
                    pieces.append(tab)
                rows_bias.append(jnp.concatenate(pieces, axis=1))
            bias = jnp.concatenate(rows_bias, axis=0)
            qm = jnp.where(head0 if t == 0 else ~head0, q, jnp.zeros_like(q))
            sc = _dot_nt(qm, kw) + bias
            m = jnp.max(sc, axis=-1, keepdims=True)
            p = jnp.exp(sc - m)
            den = jnp.sum(p, axis=-1, keepdims=True)
            outs.append(_dot(p.astype(BF16), vw) / den)
        o_ref[r0 * GRID_W:r0 * GRID_W + tq, :] = jnp.where(head0, outs[0], outs[1]).astype(BF16)


def _na_attention(bias, q, k, v):
    b = q.shape[0]
    outs = []
    for pair in range(2):
        spec = pl.BlockSpec((None, None, SEQ, LANES), lambda i, pair=pair: (i, pair, 0, 0))
        outs.append(pl.pallas_call(
            functools.partial(_na_kernel, pair=pair),
            grid=(b,),
            in_specs=[pl.BlockSpec((4 * 16, GRID_W, LANES), lambda i: (0, 0, 0)), spec, spec, spec],
            out_specs=pl.BlockSpec((None, SEQ, LANES), lambda i: (i, 0, 0)),
            out_shape=jax.ShapeDtypeStruct((b, SEQ, LANES), BF16),
            compiler_params=_params(("parallel",)),
            name=f"na_attn{pair}",
        )(bias, q, k, v))
    return outs


def _oproj_kernel(h_ref, oa1_ref, oa4_ref, oa16_ref, l1_ref, l4_ref, l16_ref,
                  ob_ref, oc_ref, od0_ref, od1_ref, wo_ref, g_ref, h1_ref, hn_ref, mix_ref):
    l1, l4, l16 = l1_ref[...], l4_ref[...], l16_ref[...]
    m = jnp.maximum(jnp.maximum(l1, l4), l16)
    e1, e4, e16 = jnp.exp(l1 - m), jnp.exp(l4 - m), jnp.exp(l16 - m)
    den = e1 + e4 + e16
    oa = ((e1 / den) * oa1_ref[...].astype(F32) + (e4 / den) * oa4_ref[...].astype(F32)
          + (e16 / den) * oa16_ref[...].astype(F32))
    mix_ref[:, 0:256] = oa.astype(BF16)
    mix_ref[:, 256:384] = ob_ref[0]
    mix_ref[:, 384:512] = ob_ref[1]
    mix_ref[:, 512:640] = oc_ref[0]
    mix_ref[:, 640:768] = oc_ref[1]
    mix_ref[:, 768:896] = od0_ref[...]
    mix_ref[:, 896:1024] = od1_ref[...]
    h1 = h_ref[...] + _dot(mix_ref[...], wo_ref[...])
    h1_ref[...] = h1
    hn_ref[...] = _rms(h1, g_ref[...], NORM_EPS).astype(BF16)


def _oproj(h, oa, la, ob, oc, od, wo, g):
    b = h.shape[0]
    tm = O_TM
    row = lambda w: pl.BlockSpec((None, tm, w), lambda i, j: (i, j, 0))
    slab = pl.BlockSpec((None, 2, tm, LANES), lambda i, j: (i, 0, j, 0))
    full = lambda shape: pl.BlockSpec(shape, lambda i, j: (0,) * len(shape))
    return pl.pallas_call(
        _oproj_kernel,
        grid=(b, SEQ // tm),
        in_specs=[row(D_MODEL), row(256), row(256), row(256), row(256), row(256), row(256),
                  slab, slab, row(LANES), row(LANES), full((D_MODEL, D_MODEL)), full((1, D_MODEL))],
        out_specs=[row(D_MODEL), row(D_MODEL)],
        out_shape=[jax.ShapeDtypeStruct((b, SEQ, D_MODEL), F32),
                   jax.ShapeDtypeStruct((b, SEQ, D_MODEL), BF16)],
        scratch_shapes=[pltpu.VMEM((tm, D_MODEL), BF16)],
        compiler_params=_params(("parallel", "parallel")),
        name="oproj",
    )(h, *oa, *la, ob, oc, *od, wo, g)


def _gelu_tanh(x):
    c = math.sqrt(2.0 / math.pi)
    return x * (0.5 * (1.0 + jnp.tanh(c * (x + 0.044715 * (x * x * x)))))


def _ffn_kernel(hn_ref, prev_ref, next_ref, h1_ref, p_ref, wup_ref, cw_ref, cb_ref, wdn_ref,
                pg_ref, wpg_ref, wpp_ref, o_ref, ext_ref, ug_ref, uv_ref, acc_ref):
    t = pl.program_id(1)
    ts, halo = FF_TS, FF_HALO
    prev = prev_ref[...]
    nxt = next_ref[...]
    ext_ref[0:halo, :] = jnp.where(t > 0, prev, jnp.zeros_like(prev))
    ext_ref[halo:halo + ts, :] = hn_ref[...]
    ext_ref[halo + ts:, :] = jnp.where(t < pl.num_programs(1) - 1, nxt, jnp.zeros_like(nxt))
    ext = ext_ref[...]

    def conv(u_ref, col):
        w = cw_ref[:, col:col + FF_CHUNK]
        return (u_ref[halo - 1:halo - 1 + ts, :] * w[0:1] + u_ref[halo:halo + ts, :] * w[1:2]
                + u_ref[halo + 1:halo + 1 + ts, :] * w[2:3] + cb_ref[:, col:col + FF_CHUNK])

    for c in range(D_FF // FF_CHUNK):
        gcol = c * FF_CHUNK
        vcol = D_FF + c * FF_CHUNK
        ug_ref[...] = _dot(ext, wup_ref[:, gcol:gcol + FF_CHUNK])
        uv_ref[...] = _dot(ext, wup_ref[:, vcol:vcol + FF_CHUNK])
        a = (_gelu_tanh(conv(ug_ref, gcol)) * conv(uv_ref, vcol)).astype(BF16)
        part = _dot(a, wdn_ref[gcol:gcol + FF_CHUNK, :])
        if c == 0:
            acc_ref[...] = part
        else:
            acc_ref[...] += part

    h2 = h1_ref[...] + acc_ref[...]
    z = _dot(_rms(h2, pg_ref[...], NORM_EPS).astype(BF16), wpg_ref[...])
    gate = 1.0 / (1.0 + jnp.exp(-z))
    e = _dot(p_ref[...].astype(BF16), wpp_ref[...])
    o_ref[...] = h2 + gate * e


def _ffn(hn, h1, p, wup, cw, cb, wdn, pg, wpg, wpp):
    b = hn.shape[0]
    ts, halo = FF_TS, FF_HALO
    nt = SEQ // ts
    per = ts // halo
    last = SEQ // halo - 1
    row = lambda w: pl.BlockSpec((None, ts, w), lambda i, j: (i, j, 0))
    full = lambda shape: pl.BlockSpec(shape, lambda i, j: (0,) * len(shape),
                                      pipeline_mode=pl.Buffered(1))
    return pl.pallas_call(
        _ffn_kernel,
        grid=(b, nt),
        in_specs=[row(D_MODEL),
                  pl.BlockSpec((None, halo, D_MODEL), lambda i, j: (i, jnp.maximum(j * per - 1, 0), 0)),
                  pl.BlockSpec((None, halo, D_MODEL), lambda i, j: (i, jnp.minimum((j + 1) * per, last), 0)),
                  row(D_MODEL), row(PLE_DIM),
                  full((D_MODEL, 2 * D_FF)), full((3, 2 * D_FF)), full((1, 2 * D_FF)),
                  full((D_FF, D_MODEL)), full((1, D_MODEL)), full((D_MODEL, D_MODEL)),
                  full((PLE_DIM, D_MODEL))],
        out_specs=row(D_MODEL),
        out_shape=jax.ShapeDtypeStruct((b, SEQ, D_MODEL), F32),
        scratch_shapes=[pltpu.VMEM((ts + 2 * halo, D_MODEL), BF16),
                        pltpu.VMEM((ts + 2 * halo, FF_CHUNK), F32),
                        pltpu.VMEM((ts + 2 * halo, FF_CHUNK), F32),
                        pltpu.VMEM((ts, D_MODEL), F32)],
        compiler_params=_params(("parallel", "parallel")),
        name="ffn_ple",
    )(hn, hn, hn, h1, p, wup, cw, cb, wdn, pg, wpg, wpp)


def _final_norm_kernel(x_ref, g_ref, o_ref):
    o_ref[...] = _rms(x_ref[...], g_ref[...], NORM_EPS)


def _final_norm(h, g):
    b = h.shape[0]
    tm = 1024
    spec = pl.BlockSpec((None, tm, D_MODEL), lambda i, j: (i, j, 0))
    return pl.pallas_call(
        _final_norm_kernel,
        grid=(b, SEQ // tm),
        in_specs=[spec, pl.BlockSpec((1, D_MODEL), lambda i, j: (0, 0))],
        out_specs=spec,
        out_shape=jax.ShapeDtypeStruct(h.shape, F32),
        compiler_params=_params(("parallel", "parallel")),
        name="final_norm",
    )(h, g)


def _prep_win(w):
    a, bq, bkv, bkr, c, d = jnp.split(w, [768, 1152, 1408, 1440, 2208], axis=1)
    scale = HEAD_DIM ** -0.5
    a = jnp.concatenate([a[:, :256] * scale, a[:, 256:]], axis=1)
    d = jnp.concatenate([d[:, :256] * scale, d[:, 256:]], axis=1)
    z = lambda n: jnp.zeros((w.shape[0], n), w.dtype)
    kr = jnp.concatenate([z(MLA_NOPE_DIM), bkr, z(LANES - MLA_NOPE_DIM - MLA_ROPE_DIM)], axis=1)
    return jnp.concatenate([a, bq, bkv, kr, c, d], axis=1).astype(BF16)


def _prep_wuq(w):
    w = w.reshape(MLA_Q_RANK, 4, MLA_NOPE_DIM + MLA_ROPE_DIM)
    w = jnp.pad(w, ((0, 0), (0, 0), (0, LANES - MLA_NOPE_DIM - MLA_ROPE_DIM)))
    return w.reshape(MLA_Q_RANK, 4 * LANES).astype(BF16)


def _prep_wukv(w):
    w = w.reshape(MLA_KV_RANK, 4, MLA_NOPE_DIM + HEAD_DIM)
    kn = jnp.pad(w[:, :, :MLA_NOPE_DIM], ((0, 0), (0, 0), (0, LANES - MLA_NOPE_DIM)))
    v = w[:, :, MLA_NOPE_DIM:]
    return jnp.concatenate([kn.reshape(MLA_KV_RANK, 4 * LANES),
                            v.reshape(MLA_KV_RANK, 4 * HEAD_DIM)], axis=1).astype(BF16)


def kernel(x, p, attn_norm, w_in, mla_q_norm, w_uq, mla_kv_norm, w_ukv, lam_q1, lam_k1, lam_q2, lam_k2,
           diff_subln, na_rpb, w_o, ffn_norm, w_up, conv_w, conv_b, w_down, ple_norm, w_ple_gate,
           w_ple_proj, final_norm):
    mla_scale = (MLA_NOPE_DIM + MLA_ROPE_DIM) ** -0.5
    diff_scale = DIFF_QK_DIM ** -0.5
    tabs = (_rope_tables(HEAD_DIM, (0, 64), 1.0),
            _rope_tables(MLA_ROPE_DIM, (64,), mla_scale),
            _rope_tables(MLA_ROPE_DIM, (64,), 1.0),
            _rope_tables(DIFF_QK_DIM, (0, 32, 64, 96), diff_scale),
            _rope_tables(DIFF_QK_DIM, (0, 32, 64, 96), 1.0))
    row = lambda v: v.reshape(1, -1)
    h = x
    for i in range(DEPTH):
        aq, ak, av, bq, bk, bv, cq, ck, cv, dq, dk, dv = _inproj(
            h, row(attn_norm[i]), _prep_win(w_in[i]), row(mla_q_norm[i]), _prep_wuq(w_uq[i]),
            row(mla_kv_norm[i]), _prep_wukv(w_ukv[i]), tabs)
        oa, la = zip(*[_band_attention(aq, ak, av, dil) for _, dil in DILATED_PATTERNS])
        ob = _mla_attention(bq, bk, bv)
        lam_init = 0.8 - 0.6 * math.exp(-0.3 * i)
        lam_vecs = jnp.stack([lam_q1[i], lam_k1[i], lam_q2[i], lam_k2[i]])
        oc = _diff_attention(lam_vecs, row(jnp.tile(diff_subln[i], 2)), cq, ck, cv, lam_init)
        od = _na_attention(_na_bias(na_rpb[i]), dq, dk, dv)
        h1, hn = _oproj(h, oa, la, ob, oc, od, w_o[i].astype(BF16), row(ffn_norm[i]))
        h = _ffn(hn, h1, p[i], w_up[i].astype(BF16), conv_w[i], row(conv_b[i]),
                 w_down[i].astype(BF16), row(ple_norm[i]), w_ple_gate[i].astype(BF16),
                 w_ple_proj[i].astype(BF16))
    return _final_norm(h, row(final_norm))
```

```python
import functools
import math

import numpy as np
import jax
import jax.numpy as jnp
from jax import lax
from jax.experimental import pallas as pl
from jax.experimental.pallas import tpu as pltpu

D_MODEL = 1024
SEQ = 2048
DEPTH = 4
GROUP_WIDTH = 256
HEAD_DIM = 64
LANES = 128
RADIUS = 64
MLA_Q_RANK = 384
MLA_KV_RANK = 256
MLA_NOPE_DIM = 64
MLA_ROPE_DIM = 32
DIFF_QK_DIM = 32
DIFF_EPS = 1e-5
GRID_W = 64
NA_WIN_ROWS = 8
NA_WIN_COLS = 16
D_FF = 2816
PLE_DIM = 256
ROPE_THETA = 10000.0
NORM_EPS = 1e-6
NEG_INF = -1e30
LOG2E = math.log2(math.e)

F32 = jnp.float32
BF16 = jnp.bfloat16

IN_A = 0
IN_CQ = 768
IN_CKV = 1152
IN_KR = 1408
IN_C = 1536
IN_D = 2304
IN_COLS_P = 3072

VMEM_LIMIT = 56 * 1024 * 1024

IN_TM = 512
ATT_TQ = 256
BAND_TQ = 512
BAND_TK = BAND_TQ + 2 * RADIUS
DIL = 4
NA_ROWS = 4
NA_KROWS = 12
FF_TS = 512
FF_GROUPS = 8
FF_GP = FF_TS // FF_GROUPS
FF_HALO = 16
FF_CHUNK = 256


def _dot(a, b):
    return jnp.dot(a, b, preferred_element_type=F32)


def _dot_nt(a, b):
    return lax.dot_general(a, b, (((1,), (1,)), ((), ())), preferred_element_type=F32)


def _rms(x, gain, eps):
    ms = jnp.mean(x * x, axis=-1, keepdims=True)
    return x * lax.rsqrt(ms + eps) * gain


def _params(sem):
    return pltpu.CompilerParams(dimension_semantics=sem, vmem_limit_bytes=VMEM_LIMIT)


def _const_spec(shape, ngrid):
    return pl.BlockSpec(shape, lambda *_: (0,) * len(shape), pipeline_mode=pl.Buffered(1))


def _rope_tables(d, groups, scale):
    half = d // 2
    pos = jnp.arange(SEQ, dtype=jnp.int32)
    inv = jnp.power(ROPE_THETA, -jnp.arange(0, d, 2, dtype=F32) / d)
    ang = pos.astype(F32)[:, None] * inv[None, :]
    cos, sin = jnp.cos(ang), jnp.sin(ang)
    idx = np.zeros(LANES, np.int32)
    first = np.zeros(LANES, bool)
    second = np.zeros(LANES, bool)
    for g in groups:
        for j in range(d):
            idx[g + j] = j % half
            (first if j < half else second)[g + j] = True
    is_rope = first | second
    c = jnp.where(is_rope[None, :], cos[:, idx], 1.0) * scale
    sa = jnp.where(first[None, :], -sin[:, idx], 0.0) * scale
    sb = jnp.where(second[None, :], sin[:, idx], 0.0) * scale
    return jnp.stack([c, sa, sb]).astype(F32)


def _rope(x, tab_ref, half):
    return (x * tab_ref[0] + pltpu.roll(x, LANES - half, 1) * tab_ref[1]
            + pltpu.roll(x, half, 1) * tab_ref[2])


def _inproj_kernel(x_ref, g_ref, win_ref, qn_ref, wuq_ref, kvn_ref, wukv_ref,
                   ta_ref, tbq_ref, tbk_ref, tcq_ref, tck_ref,
                   aq_ref, ak_ref, av_ref, bq_ref, bk_ref, bv_ref,
                   cq_ref, ck_ref, cv_ref, dq_ref, dk_ref, dv_ref):
    hn = _rms(x_ref[...], g_ref[...], NORM_EPS).astype(BF16)

    aa = _dot(hn, win_ref[:, IN_A:IN_A + 768])
    for s in range(2):
        lo = s * LANES
        aq_ref[:, lo:lo + LANES] = _rope(aa[:, lo:lo + LANES], ta_ref, 32).astype(BF16)
        ak_ref[:, lo:lo + LANES] = _rope(aa[:, 256 + lo:256 + lo + LANES], ta_ref, 32).astype(BF16)
    av_ref[...] = aa[:, 512:768].astype(BF16)

    cq = _dot(hn, win_ref[:, IN_CQ:IN_CQ + MLA_Q_RANK])
    qb = _dot(_rms(cq, qn_ref[...], NORM_EPS).astype(BF16), wuq_ref[...])
    ckv = _dot(hn, win_ref[:, IN_CKV:IN_CKV + MLA_KV_RANK])
    kvb = _dot(_rms(ckv, kvn_ref[...], NORM_EPS).astype(BF16), wukv_ref[...])
    kr = _rope(_dot(hn, win_ref[:, IN_KR:IN_KR + LANES]), tbk_ref, 16)
    for h in range(4):
        lo = h * LANES
        bq_ref[h] = _rope(qb[:, lo:lo + LANES], tbq_ref, 16).astype(BF16)
        bk_ref[h] = (kvb[:, lo:lo + LANES] + kr).astype(BF16)
    for s in range(2):
        bv_ref[s] = kvb[:, 512 + s * LANES:512 + (s + 1) * LANES].astype(BF16)

    cc = _dot(hn, win_ref[:, IN_C:IN_C + 768])
    for s in range(2):
        lo = s * LANES
        cq_ref[s] = _rope(cc[:, lo:lo + LANES], tcq_ref, 16).astype(BF16)
        ck_ref[s] = _rope(cc[:, 256 + lo:256 + lo + LANES], tck_ref, 16).astype(BF16)
        cv_ref[s] = cc[:, 512 + lo:512 + lo + LANES].astype(BF16)

    dd = _dot(hn, win_ref[:, IN_D:IN_D + 768])
    for s in range(2):
        lo = s * LANES
        dq_ref[s] = dd[:, lo:lo + LANES].astype(BF16)
        dk_ref[s] = dd[:, 256 + lo:256 + lo + LANES].astype(BF16)
        dv_ref[s] = dd[:, 512 + lo:512 + lo + LANES].astype(BF16)


def _inproj(h, g, win, qn, wuq, kvn, wukv, tabs):
    b = h.shape[0]
    tm = IN_TM
    grid = (b, SEQ // tm)
    full = lambda shape: pl.BlockSpec(shape, lambda i, j: (0,) * len(shape))
    tab = pl.BlockSpec((3, tm, LANES), lambda i, j: (0, j, 0))
    nat = pl.BlockSpec((None, tm, 256), lambda i, j: (i, j, 0))
    slab = lambda n: pl.BlockSpec((None, n, tm, LANES), lambda i, j: (i, 0, j, 0))
    nat_s = jax.ShapeDtypeStruct((b, SEQ, 256), BF16)
    slab_s = lambda n: jax.ShapeDtypeStruct((b, n, SEQ, LANES), BF16)
    return pl.pallas_call(
        _inproj_kernel,
        grid=grid,
        in_specs=[pl.BlockSpec((None, tm, D_MODEL), lambda i, j: (i, j, 0)),
                  full((1, D_MODEL)), full((D_MODEL, IN_COLS_P)),
                  full((1, MLA_Q_RANK)), full((MLA_Q_RANK, 512)),
                  full((1, MLA_KV_RANK)), full((MLA_KV_RANK, 768)),
                  tab, tab, tab, tab, tab],
        out_specs=[nat, nat, nat, slab(4), slab(4), slab(2),
                   slab(2), slab(2), slab(2), slab(2), slab(2), slab(2)],
        out_shape=[nat_s, nat_s, nat_s, slab_s(4), slab_s(4), slab_s(2),
                   slab_s(2), slab_s(2), slab_s(2), slab_s(2), slab_s(2), slab_s(2)],
        compiler_params=_params(("parallel", "parallel")),
        name="inproj",
    )(h, g, win, qn, wuq, kvn, wukv, *tabs)


def _band_masks():
    length = SEQ // DIL
    neg = lambda ok: jnp.where(ok, 0.0, NEG_INF).astype(F32)
    d1 = (jnp.arange(BAND_TK)[None, :] - jnp.arange(BAND_TQ)[:, None])[None] \
        - RADIUS * jnp.arange(3)[:, None, None]
    d = jnp.arange(length)[None, :] - jnp.arange(length)[:, None]
    step = 16 // DIL
    return (neg(jnp.abs(d1) <= RADIUS), neg(jnp.abs(d) <= RADIUS),
            neg((d % step == 0) & (jnp.abs(d) <= RADIUS * step)))


def _softmax_pv(z, v):
    m = jnp.max(z, axis=-1, keepdims=True)
    p = jnp.exp(z - m)
    den = jnp.sum(p, axis=-1, keepdims=True)
    return _dot(p.astype(BF16), v) / den, m + jnp.log(den)


def _band_kernel(q_ref, k_ref, v_ref, m1_ref, m4_ref, m16_ref, o_ref,
                 xf_ref, qr_ref, kr_ref, vr_ref, os_ref, ls_ref):
    length = SEQ // DIL
    lane = lax.broadcasted_iota(jnp.int32, (BAND_TQ, LANES), 1)
    head0 = lane < HEAD_DIM

    for src, dst in ((q_ref, qr_ref), (k_ref, kr_ref), (v_ref, vr_ref)):
        for s in range(2):
            xf_ref[...] = src[:, s * LANES:(s + 1) * LANES].astype(F32)
            for r in range(DIL):
                dst[r, s] = xf_ref[pl.ds(r, length, stride=DIL), :].astype(BF16)

    def head_pair(q, k, v, masks):
        scores = []
        for t in range(2):
            qm = jnp.where(head0 if t == 0 else ~head0, q, jnp.zeros_like(q))
            scores.append(_dot_nt(qm, k))
        res = []
        for mask in masks:
            r0, r1 = _softmax_pv(scores[0] + mask, v), _softmax_pv(scores[1] + mask, v)
            res.append((jnp.where(head0, r0[0], r1[0]), jnp.where(head0, r0[1], r1[1])))
        return res

    def dil1_block(i, carry):
        q0 = pl.multiple_of(i * BAND_TQ, BAND_TQ)
        ws = pl.multiple_of(jnp.clip(q0 - RADIUS, 0, SEQ - BAND_TK), RADIUS)
        mask = m1_ref[(q0 - ws) // RADIUS]
        for s in range(2):
            lo = s * LANES
            (o, lse), = head_pair(q_ref[pl.ds(q0, BAND_TQ), lo:lo + LANES],
                                  k_ref[pl.ds(ws, BAND_TK), lo:lo + LANES],
                                  v_ref[pl.ds(ws, BAND_TK), lo:lo + LANES], (mask,))
            os_ref[0, s, pl.ds(q0, BAND_TQ), :] = o
            ls_ref[0, s, pl.ds(q0, BAND_TQ), :] = lse
        return carry

    lax.fori_loop(0, SEQ // BAND_TQ, dil1_block, 0)

    def residue_block(r, carry):
        rows = pl.ds(r, length, stride=DIL)
        for s in range(2):
            res = head_pair(qr_ref[r, s], kr_ref[r, s], vr_ref[r, s], (m4_ref[...], m16_ref[...]))
            for pat in range(2):
                os_ref[1 + pat, s, rows, :] = res[pat][0]
                ls_ref[1 + pat, s, rows, :] = res[pat][1]
        return carry

    lax.fori_loop(0, DIL, residue_block, 0)

    def mix_block(i, carry):
        rows = pl.ds(pl.multiple_of(i * ATT_TQ, ATT_TQ), ATT_TQ)
        for s in range(2):
            l1, l4, l16 = ls_ref[0, s, rows, :], ls_ref[1, s, rows, :], ls_ref[2, s, rows, :]
            m = jnp.maximum(jnp.maximum(l1, l4), l16)
            e1, e4, e16 = jnp.exp(l1 - m), jnp.exp(l4 - m), jnp.exp(l16 - m)
            den = e1 + e4 + e16
            o_ref[rows, s * LANES:(s + 1) * LANES] = (
                (e1 / den) * os_ref[0, s, rows, :] + (e4 / den) * os_ref[1, s, rows, :]
                + (e16 / den) * os_ref[2, s, rows, :]).astype(BF16)
        return carry

    lax.fori_loop(0, SEQ // ATT_TQ, mix_block, 0)


def _band_attention(q, k, v, masks):
    b = q.shape[0]
    length = SEQ // DIL
    spec = pl.BlockSpec((None, SEQ, 256), lambda i: (i, 0, 0))
    return pl.pallas_call(
        _band_kernel,
        grid=(b,),
        in_specs=[spec, spec, spec, _const_spec((3, BAND_TQ, BAND_TK), 1),
                  _const_spec((length, length), 1), _const_spec((length, length), 1)],
        out_specs=spec,
        out_shape=jax.ShapeDtypeStruct((b, SEQ, 256), BF16),
        scratch_shapes=[pltpu.VMEM((SEQ, LANES), F32),
                        pltpu.VMEM((DIL, 2, length, LANES), BF16),
                        pltpu.VMEM((DIL, 2, length, LANES), BF16),
                        pltpu.VMEM((DIL, 2, length, LANES), BF16),
                        pltpu.VMEM((3, 2, SEQ, LANES), F32),
                        pltpu.VMEM((3, 2, SEQ, LANES), F32)],
        compiler_params=_params(("parallel",)),
        name="band_mix",
    )(q, k, v, *masks)


def _pipelined_blocks(nblk, scores, consume, s0_ref, s1_ref):
    scores(0, s0_ref)

    def body(j, carry):
        b0 = 2 * j
        scores(b0 + 1, s1_ref)
        consume(b0, s0_ref)
        scores(jnp.minimum(b0 + 2, nblk - 1), s0_ref)
        consume(b0 + 1, s1_ref)
        return carry

    lax.fori_loop(0, nblk // 2, body, 0)


def _mla_kernel(q_ref, k_ref, v_ref, o_ref, s0_ref, s1_ref):
    tq = ATT_TQ
    head0 = lax.broadcasted_iota(jnp.int32, (tq, LANES), 1) < HEAD_DIM

    def scores(blk, s_ref):
        rows = pl.ds(pl.multiple_of(blk * tq, tq), tq)
        for t in range(2):
            s_ref[t] = _dot_nt(q_ref[t, rows, :], k_ref[t])

    def consume(blk, s_ref):
        rows = pl.ds(pl.multiple_of(blk * tq, tq), tq)
        outs = []
        for t in range(2):
            sc = s_ref[t]
            p = jnp.exp2(sc - jnp.max(sc, axis=-1, keepdims=True))
            den = jnp.sum(p, axis=-1, keepdims=True)
            outs.append(_dot(p.astype(BF16), v_ref[...]) / den)
        o_ref[rows, :] = jnp.where(head0, outs[0], outs[1]).astype(BF16)

    _pipelined_blocks(SEQ // tq, scores, consume, s0_ref, s1_ref)


def _mla_attention(q, k, v):
    b = q.shape[0]
    return pl.pallas_call(
        _mla_kernel,
        grid=(b, 2),
        in_specs=[pl.BlockSpec((None, 2, SEQ, LANES), lambda i, s: (i, s, 0, 0)),
                  pl.BlockSpec((None, 2, SEQ, LANES), lambda i, s: (i, s, 0, 0)),
                  pl.BlockSpec((None, None, SEQ, LANES), lambda i, s: (i, s, 0, 0))],
        out_specs=pl.BlockSpec((None, None, SEQ, LANES), lambda i, s: (i, s, 0, 0)),
        out_shape=jax.ShapeDtypeStruct((b, 2, SEQ, LANES), BF16),
        scratch_shapes=[pltpu.VMEM((2, ATT_TQ, SEQ), F32), pltpu.VMEM((2, ATT_TQ, SEQ), F32)],
        compiler_params=_params(("parallel", "parallel")),
        name="mla_attn",
    )(q, k, v)


def _diff_kernel(lam_ref, sub_ref, q_ref, k_ref, v_ref, o_ref, s0_ref, s1_ref, *, lam_init):
    tq = ATT_TQ
    lv = lam_ref[...]
    lam = (jnp.exp(jnp.sum(lv[0:1] * lv[1:2], axis=-1, keepdims=True))
           - jnp.exp(jnp.sum(lv[2:3] * lv[3:4], axis=-1, keepdims=True)) + lam_init)
    lane = lax.broadcasted_iota(jnp.int32, (tq, LANES), 1)
    head0 = lane < HEAD_DIM

    def scores(blk, s_ref):
        q = q_ref[pl.ds(pl.multiple_of(blk * tq, tq), tq), :]
        for u in range(4):
            lo = u * DIFF_QK_DIM
            qm = jnp.where((lane >= lo) & (lane < lo + DIFF_QK_DIM), q, jnp.zeros_like(q))
            s_ref[u] = _dot_nt(qm, k_ref[...])

    def consume(blk, s_ref):
        rows = pl.ds(pl.multiple_of(blk * tq, tq), tq)
        outs = []
        for t in range(2):
            s1, s2 = s_ref[2 * t], s_ref[2 * t + 1]
            p1 = jnp.exp2(s1 - jnp.max(s1, axis=-1, keepdims=True))
            p2 = jnp.exp2(s2 - jnp.max(s2, axis=-1, keepdims=True))
            d1 = jnp.sum(p1, axis=-1, keepdims=True)
            d2 = jnp.sum(p2, axis=-1, keepdims=True)
            pn = p1 - (lam * d1 / d2) * p2
            outs.append(_dot(pn.astype(BF16), v_ref[...]) / d1)
        o = jnp.where(head0, outs[0], outs[1])
        o2 = o * o
        ss0 = jnp.sum(jnp.where(head0, o2, 0.0), axis=-1, keepdims=True)
        ss1 = jnp.sum(jnp.where(head0, 0.0, o2), axis=-1, keepdims=True)
        ms = jnp.where(head0, ss0, ss1) * (1.0 / HEAD_DIM)
        y = o * lax.rsqrt(ms + DIFF_EPS) * sub_ref[...]
        o_ref[rows, :] = (y * (1.0 - lam_init)).astype(BF16)

    _pipelined_blocks(SEQ // tq, scores, consume, s0_ref, s1_ref)


def _diff_attention(lam_vecs, subln2, q, k, v, lam_init):
    b = q.shape[0]
    spec = pl.BlockSpec((None, None, SEQ, LANES), lambda i, s: (i, s, 0, 0))
    return pl.pallas_call(
        functools.partial(_diff_kernel, lam_init=lam_init),
        grid=(b, 2),
        in_specs=[pl.BlockSpec((4, DIFF_QK_DIM), lambda i, s: (0, 0)),
                  pl.BlockSpec((1, LANES), lambda i, s: (0, 0)),
                  spec, spec, spec],
        out_specs=spec,
        out_shape=jax.ShapeDtypeStruct((b, 2, SEQ, LANES), BF16),
        scratch_shapes=[pltpu.VMEM((4, ATT_TQ, SEQ), F32), pltpu.VMEM((4, ATT_TQ, SEQ), F32)],
        compiler_params=_params(("parallel", "parallel")),
        name="diff_attn",
    )(lam_vecs, subln2, q, k, v)


def _na_bias_kernel(rpb_ref, o_ref):
    nr = 2 * NA_WIN_ROWS - 1
    nc = 2 * NA_WIN_COLS - 1
    qc = lax.broadcasted_iota(jnp.int32, (GRID_W, LANES), 0)
    lane = lax.broadcasted_iota(jnp.int32, (GRID_W, LANES), 1)
    kc = lane % GRID_W
    left = lane < GRID_W
    start = jnp.clip(qc - NA_WIN_COLS // 2, 0, GRID_W - NA_WIN_COLS)
    col_ok = (kc >= start) & (kc < start + NA_WIN_COLS)
    idx_c = jnp.clip(kc - qc, -(NA_WIN_COLS - 1), NA_WIN_COLS - 1) + (NA_WIN_COLS - 1)
    sel = [idx_c == t for t in range(nc)]

    def body(i, carry):
        h = i // 16
        d = i % 16 - 8
        acc = jnp.zeros((GRID_W, LANES), F32)
        r_lo = jnp.clip(d + 7, 0, nr - 1)
        r_hi = jnp.clip(d + 8, 0, nr - 1)
        for t in range(nc):
            v_lo = rpb_ref[(h * nr + r_lo) * nc + t]
            v_hi = rpb_ref[(h * nr + r_hi) * nc + t]
            acc = acc + jnp.where(sel[t], jnp.where(left, v_lo, v_hi), 0.0)
        dr = jnp.where(left, d, d + 1)
        ok = col_ok & (dr >= -(NA_WIN_ROWS - 1)) & (dr <= NA_WIN_ROWS - 1)
        o_ref[i] = jnp.where(ok, acc, NEG_INF)
        return carry

    lax.fori_loop(0, 4 * 16, body, 0)


def _na_bias(rpb):
    return pl.pallas_call(
        _na_bias_kernel,
        in_specs=[pl.BlockSpec(memory_space=pltpu.SMEM)],
        out_specs=pl.BlockSpec(memory_space=pltpu.VMEM),
        out_shape=jax.ShapeDtypeStruct((4 * 16, GRID_W, LANES), F32),
        name="na_bias",
    )(rpb.reshape(-1))


def _na_kernel(bias_ref, q_ref, k_ref, v_ref, o_ref, *, pair):
    rows = SEQ // GRID_W
    tq = NA_ROWS * GRID_W
    tk = NA_KROWS * GRID_W
    lane = lax.broadcasted_iota(jnp.int32, (tq, LANES), 1)
    head0 = lane < HEAD_DIM
    left = lax.broadcasted_iota(jnp.int32, (GRID_W, LANES), 1) < GRID_W
    neg = jnp.full((GRID_W, LANES), NEG_INF, F32)
    for blk in range(rows // NA_ROWS):
        r0 = blk * NA_ROWS
        ws = min(max(r0 - NA_WIN_ROWS // 2, 0), rows - NA_KROWS)
        q = q_ref[r0 * GRID_W:r0 * GRID_W + tq, :]
        kw = k_ref[ws * GRID_W:ws * GRID_W + tk, :]
        vw = v_ref[ws * GRID_W:ws * GRID_W + tk, :]
        outs = []
        for t in range(2):
            h = 2 * pair + t
            rows_bias = []
            for ri in range(NA_ROWS):
                r = r0 + ri
                lo_r = min(max(r - NA_WIN_ROWS // 2, 0), rows - NA_WIN_ROWS) - r
                pieces = []
                for j2 in range(NA_KROWS // 2):
                    d = ws + 2 * j2 - r
                    ok_l = lo_r <= d <= lo_r + NA_WIN_ROWS - 1
                    ok_r = lo_r <= d + 1 <= lo_r + NA_WIN_ROWS - 1
                    if not (ok_l or ok_r):
                        pieces.append(neg)
                        continue
                    tab = bias_ref[h * 16 + d + 8]
                    if not ok_l:
                        tab = jnp.where(left, NEG_INF, tab)
                    elif not ok_r:
                        tab = jnp.where(left, tab, NEG_INF)
                    pieces.append(tab)
                rows_bias.append(jnp.concatenate(pieces, axis=1))
            bias = jnp.concatenate(rows_bias, axis=0)
            qm = jnp.where(head0 if t == 0 else ~head0, q, jnp.zeros_like(q))
            sc = _dot_nt(qm, kw) + bias
            m = jnp.max(sc, axis=-1, keepdims=True)
            p = jnp.exp(sc - m)
            den = jnp.sum(p, axis=-1, keepdims=True)
            outs.append(_dot(p.astype(BF16), vw) / den)
        o_ref[r0 * GRID_W:r0 * GRID_W + tq, :] = jnp.where(head0, outs[0], outs[1]).astype(BF16)


def _na_attention(bias, q, k, v):
    b = q.shape[0]
    outs = []
    for pair in range(2):
        spec = pl.BlockSpec((None, None, SEQ, LANES), lambda i, pair=pair: (i, pair, 0, 0))
        outs.append(pl.pallas_call(
            functools.partial(_na_kernel, pair=pair),
            grid=(b,),
            in_specs=[pl.BlockSpec((4 * 16, GRID_W, LANES), lambda i: (0, 0, 0)), spec, spec, spec],
            out_specs=pl.BlockSpec((None, SEQ, LANES), lambda i: (i, 0, 0)),
            out_shape=jax.ShapeDtypeStruct((b, SEQ, LANES), BF16),
            compiler_params=_params(("parallel",)),
            name=f"na_attn{pair}",
        )(bias, q, k, v))
    return outs


def _oproj_kernel(h_ref, oa_ref, ob_ref, oc_ref, od0_ref, od1_ref, wo_ref, g_ref,
                  h1_ref, hnp_ref, mix_ref, hs_ref):
    mix_ref[:, 0:256] = oa_ref[...]
    mix_ref[:, 256:384] = ob_ref[0]
    mix_ref[:, 384:512] = ob_ref[1]
    mix_ref[:, 512:640] = oc_ref[0]
    mix_ref[:, 640:768] = oc_ref[1]
    mix_ref[:, 768:896] = od0_ref[...]
    mix_ref[:, 896:1024] = od1_ref[...]
    h1 = h_ref[...] + _dot(mix_ref[...], wo_ref[...])
    h1_ref[...] = h1
    hn = _rms(h1, g_ref[...], NORM_EPS)
    for cb in range(D_MODEL // LANES):
        lo = cb * LANES
        hs_ref[cb] = hn[:, lo:lo + LANES]
        for g in range(FF_GROUPS):
            hnp_ref[g, :, lo:lo + LANES] = hs_ref[cb, pl.ds(g, FF_GP, stride=FF_GROUPS), :].astype(BF16)


def _oproj(h, oa, ob, oc, od, wo, g):
    b = h.shape[0]
    tm = FF_TS
    nt = SEQ // tm
    row = lambda w: pl.BlockSpec((None, tm, w), lambda i, j: (i, j, 0))
    slab = pl.BlockSpec((None, 2, tm, LANES), lambda i, j: (i, 0, j, 0))
    return pl.pallas_call(
        _oproj_kernel,
        grid=(b, nt),
        in_specs=[row(D_MODEL), row(256), slab, slab, row(LANES), row(LANES),
                  _const_spec((D_MODEL, D_MODEL), 2), _const_spec((1, D_MODEL), 2)],
        out_specs=[row(D_MODEL),
                   pl.BlockSpec((None, None, FF_GROUPS, FF_GP, D_MODEL), lambda i, j: (i, j, 0, 0, 0))],
        out_shape=[jax.ShapeDtypeStruct((b, SEQ, D_MODEL), F32),
                   jax.ShapeDtypeStruct((b, nt, FF_GROUPS, FF_GP, D_MODEL), BF16)],
        scratch_shapes=[pltpu.VMEM((tm, D_MODEL), BF16), pltpu.VMEM((D_MODEL // LANES, tm, LANES), F32)],
        compiler_params=_params(("parallel", "parallel")),
        name="oproj",
    )(h, oa, ob, oc, *od, wo, g)


def _gelu_tanh(x):
    c = math.sqrt(2.0 / math.pi)
    return x * (0.5 * (1.0 + jnp.tanh(c * (x + 0.044715 * (x * x * x)))))


_FF_OFF = {7: FF_HALO, 0: FF_HALO + 7 * FF_GP}
_FF_OFF.update({g: FF_HALO + g * FF_GP for g in range(1, 7)})
_FF_ROWS = FF_TS + 2 * FF_HALO


def _ffn_kernel(hnp_ref, prev_ref, next_ref, h1_ref, p_ref, wup_ref, cw_ref, cb_ref, wdn_ref,
                pg_ref, wpg_ref, wpp_ref, o_ref, ext_ref, ug_ref, uv_ref, a_ref, nat_ref):
    t = pl.program_id(1)
    gp = FF_GP
    prev = prev_ref[...]
    nxt = next_ref[...]
    ext_ref[0:FF_HALO, :] = jnp.where(t > 0, prev, jnp.zeros_like(prev))
    ext_ref[_FF_ROWS - FF_HALO:, :] = jnp.where(t < pl.num_programs(1) - 1, nxt, jnp.zeros_like(nxt))
    for g in range(FF_GROUPS):
        ext_ref[_FF_OFF[g]:_FF_OFF[g] + gp, :] = hnp_ref[g]
    ext = ext_ref[...]

    def window(u_ref, slot, g, shift):
        if shift < 0 and g == 0:
            lo = _FF_OFF[7] - 1
        elif shift > 0 and g == FF_GROUPS - 1:
            lo = _FF_OFF[0] + 1
        else:
            lo = _FF_OFF[g + shift]
        return u_ref[slot, lo:lo + gp, :]

    def conv(u_ref, slot, g, col):
        w = cw_ref[:, col:col + FF_CHUNK]
        return (window(u_ref, slot, g, -1) * w[0:1] + window(u_ref, slot, g, 0) * w[1:2]
                + window(u_ref, slot, g, 1) * w[2:3] + cb_ref[:, col:col + FF_CHUNK])

    for c in range(D_FF // FF_CHUNK):
        gcol = c * FF_CHUNK
        vcol = D_FF + c * FF_CHUNK
        slot = c % 2
        ug_ref[slot] = _dot(ext, wup_ref[:, gcol:gcol + FF_CHUNK])
        uv_ref[slot] = _dot(ext, wup_ref[:, vcol:vcol + FF_CHUNK])
        for g in range(FF_GROUPS):
            a_ref[g * gp:(g + 1) * gp, gcol:gcol + FF_CHUNK] = (
                _gelu_tanh(conv(ug_ref, slot, g, gcol)) * conv(uv_ref, slot, g, vcol)).astype(BF16)

    acc = _dot(a_ref[...], wdn_ref[...])
    for cb in range(D_MODEL // LANES):
        lo = cb * LANES
        for g in range(FF_GROUPS):
            nat_ref[cb, pl.ds(g, gp, stride=FF_GROUPS), :] = acc[g * gp:(g + 1) * gp, lo:lo + LANES]
        o_ref[:, lo:lo + LANES] = h1_ref[:, lo:lo + LANES] + nat_ref[cb]
    h2 = o_ref[...]
    z = _dot(_rms(h2, pg_ref[...], NORM_EPS).astype(BF16), wpg_ref[...])
    gate = 1.0 / (1.0 + jnp.exp(-z))
    e = _dot(p_ref[...].astype(BF16), wpp_ref[...])
    o_ref[...] = h2 + gate * e


def _ffn(hnp, h1, p, wup, cw, cb, wdn, pg, wpg, wpp):
    b = h1.shape[0]
    ts = FF_TS
    nt = SEQ // ts
    gblk = FF_GP // FF_HALO
    row = lambda w: pl.BlockSpec((None, ts, w), lambda i, j: (i, j, 0))
    full = lambda shape: _const_spec(shape, 2)
    halo = lambda imap: pl.BlockSpec((None, None, None, FF_HALO, D_MODEL), imap)
    return pl.pallas_call(
        _ffn_kernel,
        grid=(b, nt),
        in_specs=[pl.BlockSpec((None, None, FF_GROUPS, FF_GP, D_MODEL), lambda i, j: (i, j, 0, 0, 0)),
                  halo(lambda i, j: (i, jnp.maximum(j - 1, 0), FF_GROUPS - 1, gblk - 1, 0)),
                  halo(lambda i, j: (i, jnp.minimum(j + 1, nt - 1), 0, 0, 0)),
                  row(D_MODEL), row(PLE_DIM),
                  full((D_MODEL, 2 * D_FF)), full((3, 2 * D_FF)), full((1, 2 * D_FF)),
                  full((D_FF, D_MODEL)), full((1, D_MODEL)), full((D_MODEL, D_MODEL)),
                  full((PLE_DIM, D_MODEL))],
        out_specs=row(D_MODEL),
        out_shape=jax.ShapeDtypeStruct((b, SEQ, D_MODEL), F32),
        scratch_shapes=[pltpu.VMEM((_FF_ROWS, D_MODEL), BF16),
                        pltpu.VMEM((2, _FF_ROWS, FF_CHUNK), F32),
                        pltpu.VMEM((2, _FF_ROWS, FF_CHUNK), F32),
                        pltpu.VMEM((ts, D_FF), BF16),
                        pltpu.VMEM((D_MODEL // LANES, ts, LANES), F32)],
        compiler_params=_params(("parallel", "parallel")),
        name="ffn_ple",
    )(hnp, hnp, hnp, h1, p, wup, cw, cb, wdn, pg, wpg, wpp)


def _final_norm_kernel(x_ref, g_ref, o_ref):
    o_ref[...] = _rms(x_ref[...], g_ref[...], NORM_EPS)


def _final_norm(h, g):
    b = h.shape[0]
    tm = 1024
    spec = pl.BlockSpec((None, tm, D_MODEL), lambda i, j: (i, j, 0))
    return pl.pallas_call(
        _final_norm_kernel,
        grid=(b, SEQ // tm),
        in_specs=[spec, pl.BlockSpec((1, D_MODEL), lambda i, j: (0, 0))],
        out_specs=spec,
        out_shape=jax.ShapeDtypeStruct(h.shape, F32),
        compiler_params=_params(("parallel", "parallel")),
        name="final_norm",
    )(h, g)


def _prep_win(w):
    a, bq, bkv, bkr, c, d = jnp.split(w, [768, 1152, 1408, 1440, 2208], axis=1)
    scale = HEAD_DIM ** -0.5
    a = jnp.concatenate([a[:, :256] * scale, a[:, 256:]], axis=1)
    d = jnp.concatenate([d[:, :256] * scale, d[:, 256:]], axis=1)
    z = lambda n: jnp.zeros((w.shape[0], n), w.dtype)
    kr = jnp.concatenate([z(MLA_NOPE_DIM), bkr, z(LANES - MLA_NOPE_DIM - MLA_ROPE_DIM)], axis=1)
    return jnp.concatenate([a, bq, bkv, kr, c, d], axis=1).astype(BF16)


def _prep_wuq(w):
    w = w.reshape(MLA_Q_RANK, 4, MLA_NOPE_DIM + MLA_ROPE_DIM)
    w = jnp.pad(w, ((0, 0), (0, 0), (0, LANES - MLA_NOPE_DIM - MLA_ROPE_DIM)))
    return w.reshape(MLA_Q_RANK, 4 * LANES).astype(BF16)


def _prep_wukv(w):
    w = w.reshape(MLA_KV_RANK, 4, MLA_NOPE_DIM + HEAD_DIM)
    kn = jnp.pad(w[:, :, :MLA_NOPE_DIM], ((0, 0), (0, 0), (0, LANES - MLA_NOPE_DIM)))
    v = w[:, :, MLA_NOPE_DIM:]
    return jnp.concatenate([kn.reshape(MLA_KV_RANK, 4 * LANES),
                            v.reshape(MLA_KV_RANK, 4 * HEAD_DIM)], axis=1).astype(BF16)


def kernel(x, p, attn_norm, w_in, mla_q_norm, w_uq, mla_kv_norm, w_ukv, lam_q1, lam_k1, lam_q2, lam_k2,
           diff_subln, na_rpb, w_o, ffn_norm, w_up, conv_w, conv_b, w_down, ple_norm, w_ple_gate,
           w_ple_proj, final_norm):
    mla_scale = (MLA_NOPE_DIM + MLA_ROPE_DIM) ** -0.5 * LOG2E
    diff_scale = DIFF_QK_DIM ** -0.5 * LOG2E
    tabs = (_rope_tables(HEAD_DIM, (0, 64), 1.0),
            _rope_tables(MLA_ROPE_DIM, (64,), mla_scale),
            _rope_tables(MLA_ROPE_DIM, (64,), 1.0),
            _rope_tables(DIFF_QK_DIM, (0, 32, 64, 96), diff_scale),
            _rope_tables(DIFF_QK_DIM, (0, 32, 64, 96), 1.0))
    masks = _band_masks()
    row = lambda v: v.reshape(1, -1)
    h = x
    for i in range(DEPTH):
        aq, ak, av, bq, bk, bv, cq, ck, cv, dq, dk, dv = _inproj(
            h, row(attn_norm[i]), _prep_win(w_in[i]), row(mla_q_norm[i]), _prep_wuq(w_uq[i]),
            row(mla_kv_norm[i]), _prep_wukv(w_ukv[i]), tabs)
        oa = _band_attention(aq, ak, av, masks)
        ob = _mla_attention(bq, bk, bv)
        lam_init = 0.8 - 0.6 * math.exp(-0.3 * i)
        lam_vecs = jnp.stack([lam_q1[i], lam_k1[i], lam_q2[i], lam_k2[i]])
        oc = _diff_attention(lam_vecs, row(jnp.tile(diff_subln[i], 2)), cq, ck, cv, lam_init)
        od = _na_attention(_na_bias(na_rpb[i]), dq, dk, dv)
        h1, hnp = _oproj(h, oa, ob, oc, od, w_o[i].astype(BF16), row(ffn_norm[i]))
        h = _ffn(hnp, h1, p[i], w_up[i].astype(BF16), conv_w[i], row(conv_b[i]),
                 w_down[i].astype(BF16), row(ple_norm[i]), w_ple_gate[i].astype(BF16),
                 w_ple_proj[i].astype(BF16))
    return _final_norm(h, row(final_norm))
```

```python
import functools
import math

import numpy as np
import jax
import jax.numpy as jnp
from jax import lax
from jax.experimental import pallas as pl
from jax.experimental.pallas import tpu as pltpu

D_MODEL = 1024
SEQ = 2048
DEPTH = 4
GROUP_WIDTH = 256
HEAD_DIM = 64
LANES = 128
RADIUS = 64
MLA_Q_RANK = 384
MLA_KV_RANK = 256
MLA_NOPE_DIM = 64
MLA_ROPE_DIM = 32
DIFF_QK_DIM = 32
DIFF_EPS = 1e-5
GRID_W = 64
NA_WIN_ROWS = 8
NA_WIN_COLS = 16
D_FF = 2816
PLE_DIM = 256
ROPE_THETA = 10000.0
NORM_EPS = 1e-6
NEG_INF = -1e30
LOG2E = math.log2(math.e)

F32 = jnp.float32
BF16 = jnp.bfloat16

IN_A = 0
IN_CQ = 768
IN_CKV = 1152
IN_KR = 1408
IN_C = 1536
IN_D = 2304
IN_COLS_P = 3072

VMEM_LIMIT = 56 * 1024 * 1024

IN_TM = 512
ATT_TQ = 512
MIX_ROWS = 256
BAND_TQ = 512
BAND_TK = BAND_TQ + 2 * RADIUS
DIL = 4
NA_ROWS = 4
NA_KROWS = 12
FF_TS = 512
FF_GROUPS = 8
FF_GP = FF_TS // FF_GROUPS
FF_HALO = 16
FF_CHUNK = 256


def _dot(a, b):
    return jnp.dot(a, b, preferred_element_type=F32)


def _dot_nt(a, b):
    return lax.dot_general(a, b, (((1,), (1,)), ((), ())), preferred_element_type=F32)


def _rms(x, gain, eps):
    ms = jnp.mean(x * x, axis=-1, keepdims=True)
    return x * lax.rsqrt(ms + eps) * gain


def _params(sem):
    return pltpu.CompilerParams(dimension_semantics=sem, vmem_limit_bytes=VMEM_LIMIT)


def _const_spec(shape, ngrid):
    return pl.BlockSpec(shape, lambda *_: (0,) * len(shape), pipeline_mode=pl.Buffered(1))


def _rope_tables(d, groups, scale):
    half = d // 2
    pos = jnp.arange(SEQ, dtype=jnp.int32)
    inv = jnp.power(ROPE_THETA, -jnp.arange(0, d, 2, dtype=F32) / d)
    ang = pos.astype(F32)[:, None] * inv[None, :]
    cos, sin = jnp.cos(ang), jnp.sin(ang)
    idx = np.zeros(LANES, np.int32)
    first = np.zeros(LANES, bool)
    second = np.zeros(LANES, bool)
    for g in groups:
        for j in range(d):
            idx[g + j] = j % half
            (first if j < half else second)[g + j] = True
    is_rope = first | second
    c = jnp.where(is_rope[None, :], cos[:, idx], 1.0) * scale
    sa = jnp.where(first[None, :], -sin[:, idx], 0.0) * scale
    sb = jnp.where(second[None, :], sin[:, idx], 0.0) * scale
    return jnp.stack([c, sa, sb]).astype(F32)


def _rope(x, tab_ref, half):
    return (x * tab_ref[0] + pltpu.roll(x, LANES - half, 1) * tab_ref[1]
            + pltpu.roll(x, half, 1) * tab_ref[2])


def _inproj_kernel(x_ref, g_ref, win_ref, qn_ref, wuq_ref, kvn_ref, wukv_ref,
                   ta_ref, tbq_ref, tbk_ref, tcq_ref, tck_ref,
                   aq_ref, ak_ref, av_ref, bq_ref, bk_ref, bv_ref,
                   cq_ref, ck_ref, cv_ref, dq_ref, dk_ref, dv_ref):
    hn = _rms(x_ref[...], g_ref[...], NORM_EPS).astype(BF16)

    aa = _dot(hn, win_ref[:, IN_A:IN_A + 768])
    for s in range(2):
        lo = s * LANES
        aq_ref[:, lo:lo + LANES] = _rope(aa[:, lo:lo + LANES], ta_ref, 32).astype(BF16)
        ak_ref[:, lo:lo + LANES] = _rope(aa[:, 256 + lo:256 + lo + LANES], ta_ref, 32).astype(BF16)
    av_ref[...] = aa[:, 512:768].astype(BF16)

    bb = _dot(hn, win_ref[:, IN_CQ:IN_C])
    cq = bb[:, 0:MLA_Q_RANK]
    ckv = bb[:, IN_CKV - IN_CQ:IN_KR - IN_CQ]
    qb = _dot(_rms(cq, qn_ref[...], NORM_EPS).astype(BF16), wuq_ref[...])
    kvb = _dot(_rms(ckv, kvn_ref[...], NORM_EPS).astype(BF16), wukv_ref[...])
    kr = _rope(bb[:, IN_KR - IN_CQ:IN_C - IN_CQ], tbk_ref, 16)
    for h in range(4):
        lo = h * LANES
        bq_ref[h] = _rope(qb[:, lo:lo + LANES], tbq_ref, 16).astype(BF16)
        bk_ref[h] = (kvb[:, lo:lo + LANES] + kr).astype(BF16)
    for s in range(2):
        bv_ref[s] = kvb[:, 512 + s * LANES:512 + (s + 1) * LANES].astype(BF16)

    cc = _dot(hn, win_ref[:, IN_C:IN_C + 768])
    for s in range(2):
        lo = s * LANES
        cq_ref[s] = _rope(cc[:, lo:lo + LANES], tcq_ref, 16).astype(BF16)
        ck_ref[s] = _rope(cc[:, 256 + lo:256 + lo + LANES], tck_ref, 16).astype(BF16)
        cv_ref[s] = cc[:, 512 + lo:512 + lo + LANES].astype(BF16)

    dd = _dot(hn, win_ref[:, IN_D:IN_D + 768])
    for s in range(2):
        lo = s * LANES
        dq_ref[s] = dd[:, lo:lo + LANES].astype(BF16)
        dk_ref[s] = dd[:, 256 + lo:256 + lo + LANES].astype(BF16)
        dv_ref[s] = dd[:, 512 + lo:512 + lo + LANES].astype(BF16)


def _inproj(h, g, win, qn, wuq, kvn, wukv, tabs):
    b = h.shape[0]
    tm = IN_TM
    grid = (b, SEQ // tm)
    full = lambda shape: pl.BlockSpec(shape, lambda i, j: (0,) * len(shape))
    tab = pl.BlockSpec((3, tm, LANES), lambda i, j: (0, j, 0))
    nat = pl.BlockSpec((None, tm, 256), lambda i, j: (i, j, 0))
    slab = lambda n: pl.BlockSpec((None, n, tm, LANES), lambda i, j: (i, 0, j, 0))
    nat_s = jax.ShapeDtypeStruct((b, SEQ, 256), BF16)
    slab_s = lambda n: jax.ShapeDtypeStruct((b, n, SEQ, LANES), BF16)
    return pl.pallas_call(
        _inproj_kernel,
        grid=grid,
        in_specs=[pl.BlockSpec((None, tm, D_MODEL), lambda i, j: (i, j, 0)),
                  full((1, D_MODEL)), full((D_MODEL, IN_COLS_P)),
                  full((1, MLA_Q_RANK)), full((MLA_Q_RANK, 512)),
                  full((1, MLA_KV_RANK)), full((MLA_KV_RANK, 768)),
                  tab, tab, tab, tab, tab],
        out_specs=[nat, nat, nat, slab(4), slab(4), slab(2),
                   slab(2), slab(2), slab(2), slab(2), slab(2), slab(2)],
        out_shape=[nat_s, nat_s, nat_s, slab_s(4), slab_s(4), slab_s(2),
                   slab_s(2), slab_s(2), slab_s(2), slab_s(2), slab_s(2), slab_s(2)],
        compiler_params=_params(("parallel", "parallel")),
        name="inproj",
    )(h, g, win, qn, wuq, kvn, wukv, *tabs)


def _band_masks():
    length = SEQ // DIL
    neg = lambda ok: jnp.where(ok, 0.0, NEG_INF).astype(F32)
    d1 = (jnp.arange(BAND_TK)[None, :] - jnp.arange(BAND_TQ)[:, None])[None] \
        - RADIUS * jnp.arange(3)[:, None, None]
    d = jnp.arange(length)[None, :] - jnp.arange(length)[:, None]
    step = 16 // DIL
    return (neg(jnp.abs(d1) <= RADIUS), neg(jnp.abs(d) <= RADIUS),
            neg((d % step == 0) & (jnp.abs(d) <= RADIUS * step)))


def _softmax_pv(z, v):
    m = jnp.max(z, axis=-1, keepdims=True)
    p = jnp.exp(z - m)
    den = jnp.sum(p, axis=-1, keepdims=True)
    return _dot(p.astype(BF16), v) / den, m + jnp.log(den)


def _band_kernel(q_ref, k_ref, v_ref, m1_ref, m4_ref, m16_ref, o_ref,
                 xf_ref, qr_ref, kr_ref, vr_ref, os_ref, ls_ref):
    length = SEQ // DIL
    lane = lax.broadcasted_iota(jnp.int32, (BAND_TQ, LANES), 1)
    head0 = lane < HEAD_DIM

    for src, dst in ((q_ref, qr_ref), (k_ref, kr_ref), (v_ref, vr_ref)):
        for s in range(2):
            xf_ref[...] = src[:, s * LANES:(s + 1) * LANES].astype(F32)
            for r in range(DIL):
                dst[r, s] = xf_ref[pl.ds(r, length, stride=DIL), :].astype(BF16)

    def head_pair(q, k, v, masks):
        scores = []
        for t in range(2):
            qm = jnp.where(head0 if t == 0 else ~head0, q, jnp.zeros_like(q))
            scores.append(_dot_nt(qm, k))
        res = []
        for mask in masks:
            r0, r1 = _softmax_pv(scores[0] + mask, v), _softmax_pv(scores[1] + mask, v)
            res.append((jnp.where(head0, r0[0], r1[0]), jnp.where(head0, r0[1], r1[1])))
        return res

    def dil1_block(i, carry):
        q0 = pl.multiple_of(i * BAND_TQ, BAND_TQ)
        ws = pl.multiple_of(jnp.clip(q0 - RADIUS, 0, SEQ - BAND_TK), RADIUS)
        mask = m1_ref[(q0 - ws) // RADIUS]
        for s in range(2):
            lo = s * LANES
            (o, lse), = head_pair(q_ref[pl.ds(q0, BAND_TQ), lo:lo + LANES],
                                  k_ref[pl.ds(ws, BAND_TK), lo:lo + LANES],
                                  v_ref[pl.ds(ws, BAND_TK), lo:lo + LANES], (mask,))
            os_ref[0, s, pl.ds(q0, BAND_TQ), :] = o
            ls_ref[0, s, pl.ds(q0, BAND_TQ), :] = lse
        return carry

    lax.fori_loop(0, SEQ // BAND_TQ, dil1_block, 0)

    def residue_block(r, carry):
        rows = pl.ds(r, length, stride=DIL)
        for s in range(2):
            res = head_pair(qr_ref[r, s], kr_ref[r, s], vr_ref[r, s], (m4_ref[...], m16_ref[...]))
            for pat in range(2):
                os_ref[1 + pat, s, rows, :] = res[pat][0]
                ls_ref[1 + pat, s, rows, :] = res[pat][1]
        return carry

    lax.fori_loop(0, DIL, residue_block, 0)

    def mix_block(i, carry):
        rows = pl.ds(pl.multiple_of(i * MIX_ROWS, MIX_ROWS), MIX_ROWS)
        for s in range(2):
            l1, l4, l16 = ls_ref[0, s, rows, :], ls_ref[1, s, rows, :], ls_ref[2, s, rows, :]
            m = jnp.maximum(jnp.maximum(l1, l4), l16)
            e1, e4, e16 = jnp.exp(l1 - m), jnp.exp(l4 - m), jnp.exp(l16 - m)
            den = e1 + e4 + e16
            o_ref[rows, s * LANES:(s + 1) * LANES] = (
                (e1 / den) * os_ref[0, s, rows, :] + (e4 / den) * os_ref[1, s, rows, :]
                + (e16 / den) * os_ref[2, s, rows, :]).astype(BF16)
        return carry

    lax.fori_loop(0, SEQ // MIX_ROWS, mix_block, 0)


def _band_attention(q, k, v, masks):
    b = q.shape[0]
    length = SEQ // DIL
    spec = pl.BlockSpec((None, SEQ, 256), lambda i: (i, 0, 0))
    return pl.pallas_call(
        _band_kernel,
        grid=(b,),
        in_specs=[spec, spec, spec, _const_spec((3, BAND_TQ, BAND_TK), 1),
                  _const_spec((length, length), 1), _const_spec((length, length), 1)],
        out_specs=spec,
        out_shape=jax.ShapeDtypeStruct((b, SEQ, 256), BF16),
        scratch_shapes=[pltpu.VMEM((SEQ, LANES), F32),
                        pltpu.VMEM((DIL, 2, length, LANES), BF16),
                        pltpu.VMEM((DIL, 2, length, LANES), BF16),
                        pltpu.VMEM((DIL, 2, length, LANES), BF16),
                        pltpu.VMEM((3, 2, SEQ, LANES), F32),
                        pltpu.VMEM((3, 2, SEQ, LANES), F32)],
        compiler_params=_params(("parallel",)),
        name="band_mix",
    )(q, k, v, *masks)


def _pipelined_blocks(nblk, scores, consume, s0_ref, s1_ref):
    scores(0, s0_ref)

    def body(j, carry):
        b0 = 2 * j
        scores(b0 + 1, s1_ref)
        consume(b0, s0_ref)
        scores(b0 + 2, s0_ref)
        consume(b0 + 1, s1_ref)
        return carry

    lax.fori_loop(0, nblk // 2 - 1, body, 0)
    scores(nblk - 1, s1_ref)
    consume(nblk - 2, s0_ref)
    consume(nblk - 1, s1_ref)


def _block_rows(blk):
    start = blk * ATT_TQ
    if not isinstance(start, int):
        start = pl.multiple_of(start, ATT_TQ)
    return pl.ds(start, ATT_TQ)


def _mla_kernel(q_ref, k_ref, v_ref, o_ref, s0_ref, s1_ref):
    tq = ATT_TQ
    head0 = lax.broadcasted_iota(jnp.int32, (tq, LANES), 1) < HEAD_DIM

    def scores(blk, s_ref):
        rows = _block_rows(blk)
        for t in range(2):
            s_ref[t] = _dot_nt(q_ref[t, rows, :], k_ref[t])

    def consume(blk, s_ref):
        rows = _block_rows(blk)
        outs = []
        for t in range(2):
            sc = s_ref[t]
            p = jnp.exp2(sc - jnp.max(sc, axis=-1, keepdims=True))
            den = jnp.sum(p, axis=-1, keepdims=True)
            outs.append(_dot(p.astype(BF16), v_ref[...]) / den)
        o_ref[rows, :] = jnp.where(head0, outs[0], outs[1]).astype(BF16)

    _pipelined_blocks(SEQ // tq, scores, consume, s0_ref, s1_ref)


def _mla_attention(q, k, v):
    b = q.shape[0]
    return pl.pallas_call(
        _mla_kernel,
        grid=(b, 2),
        in_specs=[pl.BlockSpec((None, 2, SEQ, LANES), lambda i, s: (i, s, 0, 0)),
                  pl.BlockSpec((None, 2, SEQ, LANES), lambda i, s: (i, s, 0, 0)),
                  pl.BlockSpec((None, None, SEQ, LANES), lambda i, s: (i, s, 0, 0))],
        out_specs=pl.BlockSpec((None, None, SEQ, LANES), lambda i, s: (i, s, 0, 0)),
        out_shape=jax.ShapeDtypeStruct((b, 2, SEQ, LANES), BF16),
        scratch_shapes=[pltpu.VMEM((2, ATT_TQ, SEQ), F32), pltpu.VMEM((2, ATT_TQ, SEQ), F32)],
        compiler_params=_params(("parallel", "parallel")),
        name="mla_attn",
    )(q, k, v)


def _diff_kernel(lam_ref, sub_ref, q_ref, k_ref, v_ref, o_ref, s0_ref, s1_ref, *, lam_init):
    tq = ATT_TQ
    lv = lam_ref[...]
    lam = (jnp.exp(jnp.sum(lv[0:1] * lv[1:2], axis=-1, keepdims=True))
           - jnp.exp(jnp.sum(lv[2:3] * lv[3:4], axis=-1, keepdims=True)) + lam_init)
    lane = lax.broadcasted_iota(jnp.int32, (tq, LANES), 1)
    head0 = lane < HEAD_DIM

    def scores(blk, s_ref):
        q = q_ref[_block_rows(blk), :]
        for u in range(4):
            lo = u * DIFF_QK_DIM
            qm = jnp.where((lane >= lo) & (lane < lo + DIFF_QK_DIM), q, jnp.zeros_like(q))
            s_ref[u] = _dot_nt(qm, k_ref[...])

    def consume(blk, s_ref):
        rows = _block_rows(blk)
        outs = []
        for t in range(2):
            s1, s2 = s_ref[2 * t], s_ref[2 * t + 1]
            p1 = jnp.exp2(s1 - jnp.max(s1, axis=-1, keepdims=True))
            p2 = jnp.exp2(s2 - jnp.max(s2, axis=-1, keepdims=True))
            d1 = jnp.sum(p1, axis=-1, keepdims=True)
            d2 = jnp.sum(p2, axis=-1, keepdims=True)
            pn = p1 - (lam * d1 / d2) * p2
            outs.append(_dot(pn.astype(BF16), v_ref[...]) / d1)
        o = jnp.where(head0, outs[0], outs[1])
        o2 = o * o
        ss0 = jnp.sum(jnp.where(head0, o2, 0.0), axis=-1, keepdims=True)
        ss1 = jnp.sum(jnp.where(head0, 0.0, o2), axis=-1, keepdims=True)
        ms = jnp.where(head0, ss0, ss1) * (1.0 / HEAD_DIM)
        y = o * lax.rsqrt(ms + DIFF_EPS) * sub_ref[...]
        o_ref[rows, :] = (y * (1.0 - lam_init)).astype(BF16)

    _pipelined_blocks(SEQ // tq, scores, consume, s0_ref, s1_ref)


def _diff_attention(lam_vecs, subln2, q, k, v, lam_init):
    b = q.shape[0]
    spec = pl.BlockSpec((None, None, SEQ, LANES), lambda i, s: (i, s, 0, 0))
    return pl.pallas_call(
        functools.partial(_diff_kernel, lam_init=lam_init),
        grid=(b, 2),
        in_specs=[pl.BlockSpec((4, DIFF_QK_DIM), lambda i, s: (0, 0)),
                  pl.BlockSpec((1, LANES), lambda i, s: (0, 0)),
                  spec, spec, spec],
        out_specs=spec,
        out_shape=jax.ShapeDtypeStruct((b, 2, SEQ, LANES), BF16),
        scratch_shapes=[pltpu.VMEM((4, ATT_TQ, SEQ), F32), pltpu.VMEM((4, ATT_TQ, SEQ), F32)],
        compiler_params=_params(("parallel", "parallel")),
        name="diff_attn",
    )(lam_vecs, subln2, q, k, v)


def _na_bias_kernel(rpb_ref, o_ref):
    nr = 2 * NA_WIN_ROWS - 1
    nc = 2 * NA_WIN_COLS - 1
    qc = lax.broadcasted_iota(jnp.int32, (GRID_W, LANES), 0)
    lane = lax.broadcasted_iota(jnp.int32, (GRID_W, LANES), 1)
    kc = lane % GRID_W
    left = lane < GRID_W
    start = jnp.clip(qc - NA_WIN_COLS // 2, 0, GRID_W - NA_WIN_COLS)
    col_ok = (kc >= start) & (kc < start + NA_WIN_COLS)
    idx_c = jnp.clip(kc - qc, -(NA_WIN_COLS - 1), NA_WIN_COLS - 1) + (NA_WIN_COLS - 1)
    sel = [idx_c == t for t in range(nc)]

    def body(i, carry):
        h = i // 16
        d = i % 16 - 8
        acc = jnp.zeros((GRID_W, LANES), F32)
        r_lo = jnp.clip(d + 7, 0, nr - 1)
        r_hi = jnp.clip(d + 8, 0, nr - 1)
        for t in range(nc):
            v_lo = rpb_ref[(h * nr + r_lo) * nc + t]
            v_hi = rpb_ref[(h * nr + r_hi) * nc + t]
            acc = acc + jnp.where(sel[t], jnp.where(left, v_lo, v_hi), 0.0)
        dr = jnp.where(left, d, d + 1)
        ok = col_ok & (dr >= -(NA_WIN_ROWS - 1)) & (dr <= NA_WIN_ROWS - 1)
        o_ref[i] = jnp.where(ok, acc, NEG_INF)
        return carry

    lax.fori_loop(0, 4 * 16, body, 0)


def _na_bias(rpb):
    return pl.pallas_call(
        _na_bias_kernel,
        in_specs=[pl.BlockSpec(memory_space=pltpu.SMEM)],
        out_specs=pl.BlockSpec(memory_space=pltpu.VMEM),
        out_shape=jax.ShapeDtypeStruct((4 * 16, GRID_W, LANES), F32),
        name="na_bias",
    )(rpb.reshape(-1))


def _na_kernel(bias_ref, q_ref, k_ref, v_ref, o_ref, *, pair):
    rows = SEQ // GRID_W
    tq = NA_ROWS * GRID_W
    tk = NA_KROWS * GRID_W
    lane = lax.broadcasted_iota(jnp.int32, (tq, LANES), 1)
    head0 = lane < HEAD_DIM
    left = lax.broadcasted_iota(jnp.int32, (GRID_W, LANES), 1) < GRID_W
    neg = jnp.full((GRID_W, LANES), NEG_INF, F32)
    for blk in range(rows // NA_ROWS):
        r0 = blk * NA_ROWS
        ws = min(max(r0 - NA_WIN_ROWS // 2, 0), rows - NA_KROWS)
        q = q_ref[r0 * GRID_W:r0 * GRID_W + tq, :]
        kw = k_ref[ws * GRID_W:ws * GRID_W + tk, :]
        vw = v_ref[ws * GRID_W:ws * GRID_W + tk, :]
        outs = []
        for t in range(2):
            h = 2 * pair + t
            rows_bias = []
            for ri in range(NA_ROWS):
                r = r0 + ri
                lo_r = min(max(r - NA_WIN_ROWS // 2, 0), rows - NA_WIN_ROWS) - r
                pieces = []
                for j2 in range(NA_KROWS // 2):
                    d = ws + 2 * j2 - r
                    ok_l = lo_r <= d <= lo_r + NA_WIN_ROWS - 1
                    ok_r = lo_r <= d + 1 <= lo_r + NA_WIN_ROWS - 1
                    if not (ok_l or ok_r):
                        pieces.append(neg)
                        continue
                    tab = bias_ref[h * 16 + d + 8]
                    if not ok_l:
                        tab = jnp.where(left, NEG_INF, tab)
                    elif not ok_r:
                        tab = jnp.where(left, tab, NEG_INF)
                    pieces.append(tab)
                rows_bias.append(jnp.concatenate(pieces, axis=1))
            bias = jnp.concatenate(rows_bias, axis=0)
            qm = jnp.where(head0 if t == 0 else ~head0, q, jnp.zeros_like(q))
            sc = _dot_nt(qm, kw) + bias
            m = jnp.max(sc, axis=-1, keepdims=True)
            p = jnp.exp(sc - m)
            den = jnp.sum(p, axis=-1, keepdims=True)
            outs.append(_dot(p.astype(BF16), vw) / den)
        o_ref[r0 * GRID_W:r0 * GRID_W + tq, :] = jnp.where(head0, outs[0], outs[1]).astype(BF16)


def _na_attention(bias, q, k, v):
    b = q.shape[0]
    outs = []
    for pair in range(2):
        spec = pl.BlockSpec((None, None, SEQ, LANES), lambda i, pair=pair: (i, pair, 0, 0))
        outs.append(pl.pallas_call(
            functools.partial(_na_kernel, pair=pair),
            grid=(b,),
            in_specs=[pl.BlockSpec((4 * 16, GRID_W, LANES), lambda i: (0, 0, 0)), spec, spec, spec],
            out_specs=pl.BlockSpec((None, SEQ, LANES), lambda i: (i, 0, 0)),
            out_shape=jax.ShapeDtypeStruct((b, SEQ, LANES), BF16),
            compiler_params=_params(("parallel",)),
            name=f"na_attn{pair}",
        )(bias, q, k, v))
    return outs


def _oproj_kernel(h_ref, oa_ref, ob_ref, oc_ref, od0_ref, od1_ref, wo_ref, g_ref,
                  h1_ref, hnp_ref, mix_ref, hs_ref):
    mix_ref[:, 0:256] = oa_ref[...]
    mix_ref[:, 256:384] = ob_ref[0]
    mix_ref[:, 384:512] = ob_ref[1]
    mix_ref[:, 512:640] = oc_ref[0]
    mix_ref[:, 640:768] = oc_ref[1]
    mix_ref[:, 768:896] = od0_ref[...]
    mix_ref[:, 896:1024] = od1_ref[...]
    h1 = h_ref[...] + _dot(mix_ref[...], wo_ref[...])
    h1_ref[...] = h1
    hn = _rms(h1, g_ref[...], NORM_EPS)
    for cb in range(D_MODEL // LANES):
        lo = cb * LANES
        hs_ref[cb] = hn[:, lo:lo + LANES]
        for g in range(FF_GROUPS):
            hnp_ref[g, :, lo:lo + LANES] = hs_ref[cb, pl.ds(g, FF_GP, stride=FF_GROUPS), :].astype(BF16)


def _oproj(h, oa, ob, oc, od, wo, g):
    b = h.shape[0]
    tm = FF_TS
    nt = SEQ // tm
    row = lambda w: pl.BlockSpec((None, tm, w), lambda i, j: (i, j, 0))
    slab = pl.BlockSpec((None, 2, tm, LANES), lambda i, j: (i, 0, j, 0))
    return pl.pallas_call(
        _oproj_kernel,
        grid=(b, nt),
        in_specs=[row(D_MODEL), row(256), slab, slab, row(LANES), row(LANES),
                  _const_spec((D_MODEL, D_MODEL), 2), _const_spec((1, D_MODEL), 2)],
        out_specs=[row(D_MODEL),
                   pl.BlockSpec((None, None, FF_GROUPS, FF_GP, D_MODEL), lambda i, j: (i, j, 0, 0, 0))],
        out_shape=[jax.ShapeDtypeStruct((b, SEQ, D_MODEL), F32),
                   jax.ShapeDtypeStruct((b, nt, FF_GROUPS, FF_GP, D_MODEL), BF16)],
        scratch_shapes=[pltpu.VMEM((tm, D_MODEL), BF16), pltpu.VMEM((D_MODEL // LANES, tm, LANES), F32)],
        compiler_params=_params(("parallel", "parallel")),
        name="oproj",
    )(h, oa, ob, oc, *od, wo, g)


def _gelu_tanh(x):
    c = math.sqrt(2.0 / math.pi)
    return x * (0.5 * (1.0 + jnp.tanh(c * (x + 0.044715 * (x * x * x)))))


_FF_OFF = {7: FF_HALO, 0: FF_HALO + 7 * FF_GP}
_FF_OFF.update({g: FF_HALO + g * FF_GP for g in range(1, 7)})
_FF_ROWS = FF_TS + 2 * FF_HALO


def _ffn_kernel(hnp_ref, prev_ref, next_ref, h1_ref, p_ref, wup_ref, cw_ref, cb_ref, wdn_ref,
                pg_ref, wpg_ref, wpp_ref, fg_ref, o_ref, ext_ref, ug_ref, uv_ref, a_ref, nat_ref,
                *, last_layer):
    t = pl.program_id(1)
    gp = FF_GP
    prev = prev_ref[...]
    nxt = next_ref[...]
    ext_ref[0:FF_HALO, :] = jnp.where(t > 0, prev, jnp.zeros_like(prev))
    ext_ref[_FF_ROWS - FF_HALO:, :] = jnp.where(t < pl.num_programs(1) - 1, nxt, jnp.zeros_like(nxt))
    for g in range(FF_GROUPS):
        ext_ref[_FF_OFF[g]:_FF_OFF[g] + gp, :] = hnp_ref[g]
    ext = ext_ref[...]

    def window(u_ref, slot, g, shift):
        if shift < 0 and g == 0:
            lo = _FF_OFF[7] - 1
        elif shift > 0 and g == FF_GROUPS - 1:
            lo = _FF_OFF[0] + 1
        else:
            lo = _FF_OFF[g + shift]
        return u_ref[slot, lo:lo + gp, :]

    def conv(u_ref, slot, g, col):
        w = cw_ref[:, col:col + FF_CHUNK]
        return (window(u_ref, slot, g, -1) * w[0:1] + window(u_ref, slot, g, 0) * w[1:2]
                + window(u_ref, slot, g, 1) * w[2:3] + cb_ref[:, col:col + FF_CHUNK])

    for c in range(D_FF // FF_CHUNK):
        gcol = c * FF_CHUNK
        vcol = D_FF + c * FF_CHUNK
        slot = c % 2
        ug_ref[slot] = _dot(ext, wup_ref[:, gcol:gcol + FF_CHUNK])
        uv_ref[slot] = _dot(ext, wup_ref[:, vcol:vcol + FF_CHUNK])
        for g in range(FF_GROUPS):
            a_ref[g * gp:(g + 1) * gp, gcol:gcol + FF_CHUNK] = (
                _gelu_tanh(conv(ug_ref, slot, g, gcol)) * conv(uv_ref, slot, g, vcol)).astype(BF16)

    acc = _dot(a_ref[...], wdn_ref[...])
    for cb in range(D_MODEL // LANES):
        lo = cb * LANES
        for g in range(FF_GROUPS):
            nat_ref[cb, pl.ds(g, gp, stride=FF_GROUPS), :] = acc[g * gp:(g + 1) * gp, lo:lo + LANES]
        o_ref[:, lo:lo + LANES] = h1_ref[:, lo:lo + LANES] + nat_ref[cb]
    h2 = o_ref[...]
    z = _dot(_rms(h2, pg_ref[...], NORM_EPS).astype(BF16), wpg_ref[...])
    gate = 1.0 / (1.0 + jnp.exp(-z))
    e = _dot(p_ref[...].astype(BF16), wpp_ref[...])
    out = h2 + gate * e
    o_ref[...] = _rms(out, fg_ref[...], NORM_EPS) if last_layer else out


def _ffn(hnp, h1, p, wup, cw, cb, wdn, pg, wpg, wpp, final_gain, last_layer):
    b = h1.shape[0]
    ts = FF_TS
    nt = SEQ // ts
    gblk = FF_GP // FF_HALO
    row = lambda w: pl.BlockSpec((None, ts, w), lambda i, j: (i, j, 0))
    full = lambda shape: _const_spec(shape, 2)
    halo = lambda imap: pl.BlockSpec((None, None, None, FF_HALO, D_MODEL), imap)
    return pl.pallas_call(
        functools.partial(_ffn_kernel, last_layer=last_layer),
        grid=(b, nt),
        in_specs=[pl.BlockSpec((None, None, FF_GROUPS, FF_GP, D_MODEL), lambda i, j: (i, j, 0, 0, 0)),
                  halo(lambda i, j: (i, jnp.maximum(j - 1, 0), FF_GROUPS - 1, gblk - 1, 0)),
                  halo(lambda i, j: (i, jnp.minimum(j + 1, nt - 1), 0, 0, 0)),
                  row(D_MODEL), row(PLE_DIM),
                  full((D_MODEL, 2 * D_FF)), full((3, 2 * D_FF)), full((1, 2 * D_FF)),
                  full((D_FF, D_MODEL)), full((1, D_MODEL)), full((D_MODEL, D_MODEL)),
                  full((PLE_DIM, D_MODEL)), full((1, D_MODEL))],
        out_specs=row(D_MODEL),
        out_shape=jax.ShapeDtypeStruct((b, SEQ, D_MODEL), F32),
        scratch_shapes=[pltpu.VMEM((_FF_ROWS, D_MODEL), BF16),
                        pltpu.VMEM((2, _FF_ROWS, FF_CHUNK), F32),
                        pltpu.VMEM((2, _FF_ROWS, FF_CHUNK), F32),
                        pltpu.VMEM((ts, D_FF), BF16),
                        pltpu.VMEM((D_MODEL // LANES, ts, LANES), F32)],
        compiler_params=_params(("parallel", "parallel")),
        name="ffn_ple",
    )(hnp, hnp, hnp, h1, p, wup, cw, cb, wdn, pg, wpg, wpp, final_gain)


def _prep_win(w):
    a, bq, bkv, bkr, c, d = jnp.split(w, [768, 1152, 1408, 1440, 2208], axis=1)
    scale = HEAD_DIM ** -0.5
    a = jnp.concatenate([a[:, :256] * scale, a[:, 256:]], axis=1)
    d = jnp.concatenate([d[:, :256] * scale, d[:, 256:]], axis=1)
    z = lambda n: jnp.zeros((w.shape[0], n), w.dtype)
    kr = jnp.concatenate([z(MLA_NOPE_DIM), bkr, z(LANES - MLA_NOPE_DIM - MLA_ROPE_DIM)], axis=1)
    return jnp.concatenate([a, bq, bkv, kr, c, d], axis=1).astype(BF16)


def _prep_wuq(w):
    w = w.reshape(MLA_Q_RANK, 4, MLA_NOPE_DIM + MLA_ROPE_DIM)
    w = jnp.pad(w, ((0, 0), (0, 0), (0, LANES - MLA_NOPE_DIM - MLA_ROPE_DIM)))
    return w.reshape(MLA_Q_RANK, 4 * LANES).astype(BF16)


def _prep_wukv(w):
    w = w.reshape(MLA_KV_RANK, 4, MLA_NOPE_DIM + HEAD_DIM)
    kn = jnp.pad(w[:, :, :MLA_NOPE_DIM], ((0, 0), (0, 0), (0, LANES - MLA_NOPE_DIM)))
    v = w[:, :, MLA_NOPE_DIM:]
    return jnp.concatenate([kn.reshape(MLA_KV_RANK, 4 * LANES),
                            v.reshape(MLA_KV_RANK, 4 * HEAD_DIM)], axis=1).astype(BF16)


def kernel(x, p, attn_norm, w_in, mla_q_norm, w_uq, mla_kv_norm, w_ukv, lam_q1, lam_k1, lam_q2, lam_k2,
           diff_subln, na_rpb, w_o, ffn_norm, w_up, conv_w, conv_b, w_down, ple_norm, w_ple_gate,
           w_ple_proj, final_norm):
    mla_scale = (MLA_NOPE_DIM + MLA_ROPE_DIM) ** -0.5 * LOG2E
    diff_scale = DIFF_QK_DIM ** -0.5 * LOG2E
    tabs = (_rope_tables(HEAD_DIM, (0, 64), 1.0),
            _rope_tables(MLA_ROPE_DIM, (64,), mla_scale),
            _rope_tables(MLA_ROPE_DIM, (64,), 1.0),
            _rope_tables(DIFF_QK_DIM, (0, 32, 64, 96), diff_scale),
            _rope_tables(DIFF_QK_DIM, (0, 32, 64, 96), 1.0))
    masks = _band_masks()
    row = lambda v: v.reshape(1, -1)
    h = x
    for i in range(DEPTH):
        aq, ak, av, bq, bk, bv, cq, ck, cv, dq, dk, dv = _inproj(
            h, row(attn_norm[i]), _prep_win(w_in[i]), row(mla_q_norm[i]), _prep_wuq(w_uq[i]),
            row(mla_kv_norm[i]), _prep_wukv(w_ukv[i]), tabs)
        oa = _band_attention(aq, ak, av, masks)
        ob = _mla_attention(bq, bk, bv)
        lam_init = 0.8 - 0.6 * math.exp(-0.3 * i)
        lam_vecs = jnp.stack([lam_q1[i], lam_k1[i], lam_q2[i], lam_k2[i]])
        oc = _diff_attention(lam_vecs, row(jnp.tile(diff_subln[i], 2)), cq, ck, cv, lam_init)
        od = _na_attention(_na_bias(na_rpb[i]), dq, dk, dv)
        h1, hnp = _oproj(h, oa, ob, oc, od, w_o[i].astype(BF16), row(ffn_norm[i]))
        h = _ffn(hnp, h1, p[i], w_up[i].astype(BF16), conv_w[i], row(conv_b[i]),
                 w_down[i].astype(BF16), row(ple_norm[i]), w_ple_gate[i].astype(BF16),
                 w_ple_proj[i].astype(BF16), row(final_norm), last_layer=i == DEPTH - 1)
    return h
```

```python
import functools
import math

import numpy as np
import jax
import jax.numpy as jnp
from jax import lax
from jax.experimental import pallas as pl
from jax.experimental.pallas import tpu as pltpu

D_MODEL = 1024
SEQ = 2048
DEPTH = 4
GROUP_HEADS = 4
HEAD_DIM = 64
LANES = 128
RADIUS = 64
MLA_Q_RANK = 384
MLA_KV_RANK = 256
MLA_NOPE_DIM = 64
MLA_ROPE_DIM = 32
DIFF_QK_DIM = 32
DIFF_EPS = 1e-5
GRID_W = 64
NA_WIN_ROWS = 8
NA_WIN_COLS = 16
D_FF = 2816
PLE_DIM = 256
ROPE_THETA = 10000.0
NORM_EPS = 1e-6
NEG_INF = -1e30
LOG2E = math.log2(math.e)

F32 = jnp.float32
BF16 = jnp.bfloat16

IN_A = 0
IN_CQ = 768
IN_CKV = 1152
IN_KR = 1408
IN_C = 1536
IN_D = 2304
IN_COLS_P = 3072

VMEM_LIMIT = 56 * 1024 * 1024

IN_TM = 1024
ATT_TQ = 512
MIX_ROWS = 256
BAND_TQ = 512
BAND_TK = BAND_TQ + 2 * RADIUS
DIL = 4
NA_ROWS = 4
NA_KROWS = 12
NA_TABS = 2 * NA_WIN_ROWS
FF_TS = 512
FF_GROUPS = 8
FF_GP = FF_TS // FF_GROUPS
FF_HALO = 16
FF_CHUNK = 256


def _dot(a, b):
    return jnp.dot(a, b, preferred_element_type=F32)


def _dot_nt(a, b):
    return lax.dot_general(a, b, (((1,), (1,)), ((), ())), preferred_element_type=F32)


def _rms(x, gain, eps):
    ms = jnp.mean(x * x, axis=-1, keepdims=True)
    return x * lax.rsqrt(ms + eps) * gain


def _params(sem):
    return pltpu.CompilerParams(dimension_semantics=sem, vmem_limit_bytes=VMEM_LIMIT)


def _const_spec(shape):
    return pl.BlockSpec(shape, lambda *_: (0,) * len(shape), pipeline_mode=pl.Buffered(1))


def _rope_tables(d, groups, scale):
    half = d // 2
    pos = jnp.arange(SEQ, dtype=jnp.int32)
    inv = jnp.power(ROPE_THETA, -jnp.arange(0, d, 2, dtype=F32) / d)
    ang = pos.astype(F32)[:, None] * inv[None, :]
    cos, sin = jnp.cos(ang), jnp.sin(ang)
    idx = np.zeros(LANES, np.int32)
    first = np.zeros(LANES, bool)
    second = np.zeros(LANES, bool)
    for g in groups:
        for j in range(d):
            idx[g + j] = j % half
            (first if j < half else second)[g + j] = True
    is_rope = first | second
    c = jnp.where(is_rope[None, :], cos[:, idx], 1.0) * scale
    sa = jnp.where(first[None, :], -sin[:, idx], 0.0) * scale
    sb = jnp.where(second[None, :], sin[:, idx], 0.0) * scale
    return jnp.stack([c, sa, sb]).astype(F32)


def _rope(x, tab_ref, half):
    return (x * tab_ref[0] + pltpu.roll(x, LANES - half, 1) * tab_ref[1]
            + pltpu.roll(x, half, 1) * tab_ref[2])


def _inproj_kernel(x_ref, g_ref, win_ref, qn_ref, wuq_ref, kvn_ref, wukv_ref,
                   ta_ref, tbq_ref, tbk_ref, tcq_ref, tck_ref,
                   aq_ref, ak_ref, av_ref, bq_ref, bk_ref, bv_ref,
                   cq_ref, ck_ref, cv_ref, dq_ref, dk_ref, dv_ref):
    hn = _rms(x_ref[...], g_ref[...], NORM_EPS).astype(BF16)

    aa = _dot(hn, win_ref[:, IN_A:IN_A + 768])
    for s in range(2):
        lo = s * LANES
        aq_ref[:, lo:lo + LANES] = _rope(aa[:, lo:lo + LANES], ta_ref, 32).astype(BF16)
        ak_ref[:, lo:lo + LANES] = _rope(aa[:, 256 + lo:256 + lo + LANES], ta_ref, 32).astype(BF16)
    av_ref[...] = aa[:, 512:768].astype(BF16)

    bb = _dot(hn, win_ref[:, IN_CQ:IN_C])
    cq = bb[:, 0:MLA_Q_RANK]
    ckv = bb[:, IN_CKV - IN_CQ:IN_KR - IN_CQ]
    qb = _dot(_rms(cq, qn_ref[...], NORM_EPS).astype(BF16), wuq_ref[...])
    kvb = _dot(_rms(ckv, kvn_ref[...], NORM_EPS).astype(BF16), wukv_ref[...])
    kr = _rope(bb[:, IN_KR - IN_CQ:IN_C - IN_CQ], tbk_ref, 16)
    for h in range(4):
        lo = h * LANES
        bq_ref[h] = _rope(qb[:, lo:lo + LANES], tbq_ref, 16).astype(BF16)
        bk_ref[h] = (kvb[:, lo:lo + LANES] + kr).astype(BF16)
    for s in range(2):
        bv_ref[s] = kvb[:, 512 + s * LANES:512 + (s + 1) * LANES].astype(BF16)

    cc = _dot(hn, win_ref[:, IN_C:IN_C + 768])
    for s in range(2):
        lo = s * LANES
        cq_ref[s] = _rope(cc[:, lo:lo + LANES], tcq_ref, 16).astype(BF16)
        ck_ref[s] = _rope(cc[:, 256 + lo:256 + lo + LANES], tck_ref, 16).astype(BF16)
        cv_ref[s] = cc[:, 512 + lo:512 + lo + LANES].astype(BF16)

    dd = _dot(hn, win_ref[:, IN_D:IN_D + 768])
    for s in range(2):
        lo = s * LANES
        dq_ref[s] = dd[:, lo:lo + LANES].astype(BF16)
        dk_ref[s] = dd[:, 256 + lo:256 + lo + LANES].astype(BF16)
        dv_ref[s] = dd[:, 512 + lo:512 + lo + LANES].astype(BF16)


def _inproj(h, g, win, qn, wuq, kvn, wukv, tabs):
    b = h.shape[0]
    tm = IN_TM
    grid = (b, SEQ // tm)
    full = lambda shape: pl.BlockSpec(shape, lambda i, j: (0,) * len(shape))
    tab = pl.BlockSpec((3, tm, LANES), lambda i, j: (0, j, 0))
    nat = pl.BlockSpec((None, tm, 256), lambda i, j: (i, j, 0))
    slab = lambda n: pl.BlockSpec((None, n, tm, LANES), lambda i, j: (i, 0, j, 0))
    nat_s = jax.ShapeDtypeStruct((b, SEQ, 256), BF16)
    slab_s = lambda n: jax.ShapeDtypeStruct((b, n, SEQ, LANES), BF16)
    return pl.pallas_call(
        _inproj_kernel,
        grid=grid,
        in_specs=[pl.BlockSpec((None, tm, D_MODEL), lambda i, j: (i, j, 0)),
                  full((1, D_MODEL)), full((D_MODEL, IN_COLS_P)),
                  full((1, MLA_Q_RANK)), full((MLA_Q_RANK, 512)),
                  full((1, MLA_KV_RANK)), full((MLA_KV_RANK, 768)),
                  tab, tab, tab, tab, tab],
        out_specs=[nat, nat, nat, slab(4), slab(4), slab(2),
                   slab(2), slab(2), slab(2), slab(2), slab(2), slab(2)],
        out_shape=[nat_s, nat_s, nat_s, slab_s(4), slab_s(4), slab_s(2),
                   slab_s(2), slab_s(2), slab_s(2), slab_s(2), slab_s(2), slab_s(2)],
        compiler_params=_params(("parallel", "parallel")),
        name="inproj",
    )(h, g, win, qn, wuq, kvn, wukv, *tabs)


def _band_masks():
    length = SEQ // DIL
    neg = lambda ok: jnp.where(ok, 0.0, NEG_INF).astype(F32)
    d1 = (jnp.arange(BAND_TK)[None, :] - jnp.arange(BAND_TQ)[:, None])[None] \
        - RADIUS * jnp.arange(3)[:, None, None]
    d = jnp.arange(length)[None, :] - jnp.arange(length)[:, None]
    step = 16 // DIL
    return (neg(jnp.abs(d1) <= RADIUS), neg(jnp.abs(d) <= RADIUS),
            neg((d % step == 0) & (jnp.abs(d) <= RADIUS * step)))


def _softmax_pv(z, v):
    m = jnp.max(z, axis=-1, keepdims=True)
    p = jnp.exp(z - m)
    den = jnp.sum(p, axis=-1, keepdims=True)
    return _dot(p.astype(BF16), v) / den, m + jnp.log(den)


def _band_kernel(q_ref, k_ref, v_ref, m1_ref, m4_ref, m16_ref, o_ref,
                 xf_ref, qr_ref, kr_ref, vr_ref, os_ref, ls_ref):
    length = SEQ // DIL
    lane = lax.broadcasted_iota(jnp.int32, (BAND_TQ, LANES), 1)
    head0 = lane < HEAD_DIM

    for src, dst in ((q_ref, qr_ref), (k_ref, kr_ref), (v_ref, vr_ref)):
        for s in range(2):
            xf_ref[...] = src[:, s * LANES:(s + 1) * LANES].astype(F32)
            for r in range(DIL):
                dst[r, s] = xf_ref[pl.ds(r, length, stride=DIL), :].astype(BF16)

    def head_pair(q, k, v, masks):
        scores = []
        for t in range(2):
            qm = jnp.where(head0 if t == 0 else ~head0, q, jnp.zeros_like(q))
            scores.append(_dot_nt(qm, k))
        res = []
        for mask in masks:
            r0, r1 = _softmax_pv(scores[0] + mask, v), _softmax_pv(scores[1] + mask, v)
            res.append((jnp.where(head0, r0[0], r1[0]), jnp.where(head0, r0[1], r1[1])))
        return res

    def dil1_block(i, carry):
        q0 = pl.multiple_of(i * BAND_TQ, BAND_TQ)
        ws = pl.multiple_of(jnp.clip(q0 - RADIUS, 0, SEQ - BAND_TK), RADIUS)
        mask = m1_ref[(q0 - ws) // RADIUS]
        for s in range(2):
            lo = s * LANES
            (o, lse), = head_pair(q_ref[pl.ds(q0, BAND_TQ), lo:lo + LANES],
                                  k_ref[pl.ds(ws, BAND_TK), lo:lo + LANES],
                                  v_ref[pl.ds(ws, BAND_TK), lo:lo + LANES], (mask,))
            os_ref[0, s, pl.ds(q0, BAND_TQ), :] = o
            ls_ref[0, s, pl.ds(q0, BAND_TQ), :] = lse
        return carry

    lax.fori_loop(0, SEQ // BAND_TQ, dil1_block, 0)

    def residue_block(r, carry):
        rows = pl.ds(r, length, stride=DIL)
        for s in range(2):
            res = head_pair(qr_ref[r, s], kr_ref[r, s], vr_ref[r, s], (m4_ref[...], m16_ref[...]))
            for pat in range(2):
                os_ref[1 + pat, s, rows, :] = res[pat][0]
                ls_ref[1 + pat, s, rows, :] = res[pat][1]
        return carry

    lax.fori_loop(0, DIL, residue_block, 0)

    def mix_block(i, carry):
        rows = pl.ds(pl.multiple_of(i * MIX_ROWS, MIX_ROWS), MIX_ROWS)
        for s in range(2):
            l1, l4, l16 = ls_ref[0, s, rows, :], ls_ref[1, s, rows, :], ls_ref[2, s, rows, :]
            m = jnp.maximum(jnp.maximum(l1, l4), l16)
            e1, e4, e16 = jnp.exp(l1 - m), jnp.exp(l4 - m), jnp.exp(l16 - m)
            den = e1 + e4 + e16
            o_ref[rows, s * LANES:(s + 1) * LANES] = (
                (e1 / den) * os_ref[0, s, rows, :] + (e4 / den) * os_ref[1, s, rows, :]
                + (e16 / den) * os_ref[2, s, rows, :]).astype(BF16)
        return carry

    lax.fori_loop(0, SEQ // MIX_ROWS, mix_block, 0)


def _band_attention(q, k, v, masks):
    b = q.shape[0]
    length = SEQ // DIL
    spec = pl.BlockSpec((None, SEQ, 256), lambda i: (i, 0, 0))
    return pl.pallas_call(
        _band_kernel,
        grid=(b,),
        in_specs=[spec, spec, spec, _const_spec((3, BAND_TQ, BAND_TK)),
                  _const_spec((length, length)), _const_spec((length, length))],
        out_specs=spec,
        out_shape=jax.ShapeDtypeStruct((b, SEQ, 256), BF16),
        scratch_shapes=[pltpu.VMEM((SEQ, LANES), F32),
                        pltpu.VMEM((DIL, 2, length, LANES), BF16),
                        pltpu.VMEM((DIL, 2, length, LANES), BF16),
                        pltpu.VMEM((DIL, 2, length, LANES), BF16),
                        pltpu.VMEM((3, 2, SEQ, LANES), F32),
                        pltpu.VMEM((3, 2, SEQ, LANES), F32)],
        compiler_params=_params(("parallel",)),
        name="band_mix",
    )(q, k, v, *masks)


def _pipelined_blocks(nblk, scores, consume, s0_ref, s1_ref):
    scores(0, s0_ref)

    def body(j, carry):
        b0 = 2 * j
        scores(b0 + 1, s1_ref)
        consume(b0, s0_ref)
        scores(b0 + 2, s0_ref)
        consume(b0 + 1, s1_ref)
        return carry

    lax.fori_loop(0, nblk // 2 - 1, body, 0)
    scores(nblk - 1, s1_ref)
    consume(nblk - 2, s0_ref)
    consume(nblk - 1, s1_ref)


def _block_rows(blk):
    start = blk * ATT_TQ
    if not isinstance(start, int):
        start = pl.multiple_of(start, ATT_TQ)
    return pl.ds(start, ATT_TQ)


def _mla_kernel(q_ref, k_ref, v_ref, o_ref, s0_ref, s1_ref):
    tq = ATT_TQ
    head0 = lax.broadcasted_iota(jnp.int32, (tq, LANES), 1) < HEAD_DIM

    def head_pair(s, carry):
        def scores(blk, s_ref):
            rows = _block_rows(blk)
            for t in range(2):
                s_ref[t] = _dot_nt(q_ref[2 * s + t, rows, :], k_ref[2 * s + t])

        def consume(blk, s_ref):
            rows = _block_rows(blk)
            outs = []
            for t in range(2):
                sc = s_ref[t]
                p = jnp.exp2(sc - jnp.max(sc, axis=-1, keepdims=True))
                den = jnp.sum(p, axis=-1, keepdims=True)
                outs.append(_dot(p.astype(BF16), v_ref[s]) / den)
            o_ref[s, rows, :] = jnp.where(head0, outs[0], outs[1]).astype(BF16)

        _pipelined_blocks(SEQ // tq, scores, consume, s0_ref, s1_ref)
        return carry

    lax.fori_loop(0, 2, head_pair, 0)


def _mla_attention(q, k, v):
    b = q.shape[0]
    return pl.pallas_call(
        _mla_kernel,
        grid=(b,),
        in_specs=[pl.BlockSpec((None, 4, SEQ, LANES), lambda i: (i, 0, 0, 0)),
                  pl.BlockSpec((None, 4, SEQ, LANES), lambda i: (i, 0, 0, 0)),
                  pl.BlockSpec((None, 2, SEQ, LANES), lambda i: (i, 0, 0, 0))],
        out_specs=pl.BlockSpec((None, 2, SEQ, LANES), lambda i: (i, 0, 0, 0)),
        out_shape=jax.ShapeDtypeStruct((b, 2, SEQ, LANES), BF16),
        scratch_shapes=[pltpu.VMEM((2, ATT_TQ, SEQ), F32), pltpu.VMEM((2, ATT_TQ, SEQ), F32)],
        compiler_params=_params(("parallel",)),
        name="mla_attn",
    )(q, k, v)


def _diff_kernel(lam_ref, sub_ref, q_ref, k_ref, v_ref, o_ref, s0_ref, s1_ref, *, lam_init):
    tq = ATT_TQ
    lv = lam_ref[...]
    lam = (jnp.exp(jnp.sum(lv[0:1] * lv[1:2], axis=-1, keepdims=True))
           - jnp.exp(jnp.sum(lv[2:3] * lv[3:4], axis=-1, keepdims=True)) + lam_init)
    lane = lax.broadcasted_iota(jnp.int32, (tq, LANES), 1)
    head0 = lane < HEAD_DIM

    def head_pair(s, carry):
        def scores(blk, s_ref):
            q = q_ref[s, _block_rows(blk), :]
            for u in range(4):
                lo = u * DIFF_QK_DIM
                qm = jnp.where((lane >= lo) & (lane < lo + DIFF_QK_DIM), q, jnp.zeros_like(q))
                s_ref[u] = _dot_nt(qm, k_ref[s])

        def consume(blk, s_ref):
            rows = _block_rows(blk)
            outs = []
            for t in range(2):
                s1, s2 = s_ref[2 * t], s_ref[2 * t + 1]
                p1 = jnp.exp2(s1 - jnp.max(s1, axis=-1, keepdims=True))
                p2 = jnp.exp2(s2 - jnp.max(s2, axis=-1, keepdims=True))
                d1 = jnp.sum(p1, axis=-1, keepdims=True)
                d2 = jnp.sum(p2, axis=-1, keepdims=True)
                pn = p1 - (lam * d1 / d2) * p2
                outs.append(_dot(pn.astype(BF16), v_ref[s]) / d1)
            o = jnp.where(head0, outs[0], outs[1])
            o2 = o * o
            ss0 = jnp.sum(jnp.where(head0, o2, 0.0), axis=-1, keepdims=True)
            ss1 = jnp.sum(jnp.where(head0, 0.0, o2), axis=-1, keepdims=True)
            ms = jnp.where(head0, ss0, ss1) * (1.0 / HEAD_DIM)
            y = o * lax.rsqrt(ms + DIFF_EPS) * sub_ref[...]
            o_ref[s, rows, :] = (y * (1.0 - lam_init)).astype(BF16)

        _pipelined_blocks(SEQ // tq, scores, consume, s0_ref, s1_ref)
        return carry

    lax.fori_loop(0, 2, head_pair, 0)


def _diff_attention(lam_vecs, subln2, q, k, v, lam_init):
    b = q.shape[0]
    spec = pl.BlockSpec((None, 2, SEQ, LANES), lambda i: (i, 0, 0, 0))
    return pl.pallas_call(
        functools.partial(_diff_kernel, lam_init=lam_init),
        grid=(b,),
        in_specs=[pl.BlockSpec((4, DIFF_QK_DIM), lambda i: (0, 0)),
                  pl.BlockSpec((1, LANES), lambda i: (0, 0)),
                  spec, spec, spec],
        out_specs=spec,
        out_shape=jax.ShapeDtypeStruct((b, 2, SEQ, LANES), BF16),
        scratch_shapes=[pltpu.VMEM((4, ATT_TQ, SEQ), F32), pltpu.VMEM((4, ATT_TQ, SEQ), F32)],
        compiler_params=_params(("parallel",)),
        name="diff_attn",
    )(lam_vecs, subln2, q, k, v)


def _na_bias_kernel(rpb_ref, o_ref):
    nr = 2 * NA_WIN_ROWS - 1
    nc = 2 * NA_WIN_COLS - 1
    qc = lax.broadcasted_iota(jnp.int32, (GRID_W, LANES), 0)
    lane = lax.broadcasted_iota(jnp.int32, (GRID_W, LANES), 1)
    kc = lane % GRID_W
    left = lane < GRID_W
    start = jnp.clip(qc - NA_WIN_COLS // 2, 0, GRID_W - NA_WIN_COLS)
    col_ok = (kc >= start) & (kc < start + NA_WIN_COLS)
    idx_c = jnp.clip(kc - qc, -(NA_WIN_COLS - 1), NA_WIN_COLS - 1) + (NA_WIN_COLS - 1)
    sel = [idx_c == t for t in range(nc)]

    def body(i, carry):
        h = i // NA_TABS
        d = i % NA_TABS - NA_WIN_ROWS
        acc = jnp.zeros((GRID_W, LANES), F32)
        r_lo = jnp.clip(d + NA_WIN_ROWS - 1, 0, nr - 1)
        r_hi = jnp.clip(d + NA_WIN_ROWS, 0, nr - 1)
        for t in range(nc):
            v_lo = rpb_ref[(h * nr + r_lo) * nc + t]
            v_hi = rpb_ref[(h * nr + r_hi) * nc + t]
            acc = acc + jnp.where(sel[t], jnp.where(left, v_lo, v_hi), 0.0)
        dr = jnp.where(left, d, d + 1)
        ok = col_ok & (dr >= -(NA_WIN_ROWS - 1)) & (dr <= NA_WIN_ROWS - 1)
        o_ref[i] = jnp.where(ok, acc, NEG_INF)
        return carry

    lax.fori_loop(0, GROUP_HEADS * NA_TABS, body, 0)


def _na_bias(rpb):
    return pl.pallas_call(
        _na_bias_kernel,
        in_specs=[pl.BlockSpec(memory_space=pltpu.SMEM)],
        out_specs=pl.BlockSpec(memory_space=pltpu.VMEM),
        out_shape=jax.ShapeDtypeStruct((GROUP_HEADS * NA_TABS, GRID_W, LANES), F32),
        name="na_bias",
    )(rpb.reshape(-1))


def _na_kernel(bias_ref, q_ref, k_ref, v_ref, o_ref):
    rows = SEQ // GRID_W
    tq = NA_ROWS * GRID_W
    tk = NA_KROWS * GRID_W
    lane = lax.broadcasted_iota(jnp.int32, (tq, LANES), 1)
    head0 = lane < HEAD_DIM
    left = lax.broadcasted_iota(jnp.int32, (GRID_W, LANES), 1) < GRID_W
    neg = jnp.full((GRID_W, LANES), NEG_INF, F32)

    def head_pair(pair, carry):
        for blk in range(rows // NA_ROWS):
            r0 = blk * NA_ROWS
            ws = min(max(r0 - NA_WIN_ROWS // 2, 0), rows - NA_KROWS)
            q = q_ref[pair, r0 * GRID_W:r0 * GRID_W + tq, :]
            kw = k_ref[pair, ws * GRID_W:ws * GRID_W + tk, :]
            vw = v_ref[pair, ws * GRID_W:ws * GRID_W + tk, :]
            outs = []
            for t in range(2):
                tab0 = (2 * pair + t) * NA_TABS
                rows_bias = []
                for ri in range(NA_ROWS):
                    r = r0 + ri
                    lo_r = min(max(r - NA_WIN_ROWS // 2, 0), rows - NA_WIN_ROWS) - r
                    pieces = []
                    for j2 in range(NA_KROWS // 2):
                        d = ws + 2 * j2 - r
                        ok_l = lo_r <= d <= lo_r + NA_WIN_ROWS - 1
                        ok_r = lo_r <= d + 1 <= lo_r + NA_WIN_ROWS - 1
                        if not (ok_l or ok_r):
                            pieces.append(neg)
                            continue
                        tab = bias_ref[tab0 + d + NA_WIN_ROWS]
                        if not ok_l:
                            tab = jnp.where(left, NEG_INF, tab)
                        elif not ok_r:
                            tab = jnp.where(left, tab, NEG_INF)
                        pieces.append(tab)
                    rows_bias.append(jnp.concatenate(pieces, axis=1))
                bias = jnp.concatenate(rows_bias, axis=0)
                qm = jnp.where(head0 if t == 0 else ~head0, q, jnp.zeros_like(q))
                sc = _dot_nt(qm, kw) + bias
                m = jnp.max(sc, axis=-1, keepdims=True)
                p = jnp.exp(sc - m)
                den = jnp.sum(p, axis=-1, keepdims=True)
                outs.append(_dot(p.astype(BF16), vw) / den)
            o_ref[pair, r0 * GRID_W:r0 * GRID_W + tq, :] = (
                jnp.where(head0, outs[0], outs[1]).astype(BF16))
        return carry

    lax.fori_loop(0, 2, head_pair, 0)


def _na_attention(bias, q, k, v):
    b = q.shape[0]
    spec = pl.BlockSpec((None, 2, SEQ, LANES), lambda i: (i, 0, 0, 0))
    return pl.pallas_call(
        _na_kernel,
        grid=(b,),
        in_specs=[_const_spec((GROUP_HEADS * NA_TABS, GRID_W, LANES)), spec, spec, spec],
        out_specs=spec,
        out_shape=jax.ShapeDtypeStruct((b, 2, SEQ, LANES), BF16),
        compiler_params=_params(("parallel",)),
        name="na_attn",
    )(bias, q, k, v)


def _oproj_kernel(h_ref, oa_ref, ob_ref, oc_ref, od_ref, wo_ref, g_ref,
                  h1_ref, hnp_ref, mix_ref, hs_ref):
    mix_ref[:, 0:256] = oa_ref[...]
    mix_ref[:, 256:384] = ob_ref[0]
    mix_ref[:, 384:512] = ob_ref[1]
    mix_ref[:, 512:640] = oc_ref[0]
    mix_ref[:, 640:768] = oc_ref[1]
    mix_ref[:, 768:896] = od_ref[0]
    mix_ref[:, 896:1024] = od_ref[1]
    h1 = h_ref[...] + _dot(mix_ref[...], wo_ref[...])
    h1_ref[...] = h1
    hn = _rms(h1, g_ref[...], NORM_EPS)
    for cb in range(D_MODEL // LANES):
        lo = cb * LANES
        hs_ref[cb] = hn[:, lo:lo + LANES]
        for g in range(FF_GROUPS):
            hnp_ref[g, :, lo:lo + LANES] = hs_ref[cb, pl.ds(g, FF_GP, stride=FF_GROUPS), :].astype(BF16)


def _oproj(h, oa, ob, oc, od, wo, g):
    b = h.shape[0]
    tm = FF_TS
    nt = SEQ // tm
    row = lambda w: pl.BlockSpec((None, tm, w), lambda i, j: (i, j, 0))
    slab = pl.BlockSpec((None, 2, tm, LANES), lambda i, j: (i, 0, j, 0))
    return pl.pallas_call(
        _oproj_kernel,
        grid=(b, nt),
        in_specs=[row(D_MODEL), row(256), slab, slab, slab,
                  _const_spec((D_MODEL, D_MODEL)), _const_spec((1, D_MODEL))],
        out_specs=[row(D_MODEL),
                   pl.BlockSpec((None, None, FF_GROUPS, FF_GP, D_MODEL), lambda i, j: (i, j, 0, 0, 0))],
        out_shape=[jax.ShapeDtypeStruct((b, SEQ, D_MODEL), F32),
                   jax.ShapeDtypeStruct((b, nt, FF_GROUPS, FF_GP, D_MODEL), BF16)],
        scratch_shapes=[pltpu.VMEM((tm, D_MODEL), BF16), pltpu.VMEM((D_MODEL // LANES, tm, LANES), F32)],
        compiler_params=_params(("parallel", "parallel")),
        name="oproj",
    )(h, oa, ob, oc, od, wo, g)


def _gelu_tanh(x):
    c = math.sqrt(2.0 / math.pi)
    return x * (0.5 * (1.0 + jnp.tanh(c * (x + 0.044715 * (x * x * x)))))


_FF_OFF = {7: FF_HALO, 0: FF_HALO + 7 * FF_GP}
_FF_OFF.update({g: FF_HALO + g * FF_GP for g in range(1, 7)})
_FF_ROWS = FF_TS + 2 * FF_HALO


def _ffn_kernel(hnp_ref, prev_ref, next_ref, h1_ref, p_ref, wup_ref, cw_ref, cb_ref, wdn_ref,
                pg_ref, wpg_ref, wpp_ref, fg_ref, o_ref, ext_ref, ug_ref, uv_ref, a_ref, nat_ref,
                *, last_layer):
    t = pl.program_id(1)
    gp = FF_GP
    prev = prev_ref[...]
    nxt = next_ref[...]
    ext_ref[0:FF_HALO, :] = jnp.where(t > 0, prev, jnp.zeros_like(prev))
    ext_ref[_FF_ROWS - FF_HALO:, :] = jnp.where(t < pl.num_programs(1) - 1, nxt, jnp.zeros_like(nxt))
    for g in range(FF_GROUPS):
        ext_ref[_FF_OFF[g]:_FF_OFF[g] + gp, :] = hnp_ref[g]
    ext = ext_ref[...]

    def window(u_ref, slot, g, shift):
        if shift < 0 and g == 0:
            lo = _FF_OFF[7] - 1
        elif shift > 0 and g == FF_GROUPS - 1:
            lo = _FF_OFF[0] + 1
        else:
            lo = _FF_OFF[g + shift]
        return u_ref[slot, lo:lo + gp, :]

    def conv(u_ref, slot, g, col):
        w = cw_ref[:, col:col + FF_CHUNK]
        return (window(u_ref, slot, g, -1) * w[0:1] + window(u_ref, slot, g, 0) * w[1:2]
                + window(u_ref, slot, g, 1) * w[2:3] + cb_ref[:, col:col + FF_CHUNK])

    for c in range(D_FF // FF_CHUNK):
        gcol = c * FF_CHUNK
        vcol = D_FF + c * FF_CHUNK
        slot = c % 2
        ug_ref[slot] = _dot(ext, wup_ref[:, gcol:gcol + FF_CHUNK])
        uv_ref[slot] = _dot(ext, wup_ref[:, vcol:vcol + FF_CHUNK])
        for g in range(FF_GROUPS):
            a_ref[g * gp:(g + 1) * gp, gcol:gcol + FF_CHUNK] = (
                _gelu_tanh(conv(ug_ref, slot, g, gcol)) * conv(uv_ref, slot, g, vcol)).astype(BF16)

    acc = _dot(a_ref[...], wdn_ref[...])
    for cb in range(D_MODEL // LANES):
        lo = cb * LANES
        for g in range(FF_GROUPS):
            nat_ref[cb, pl.ds(g, gp, stride=FF_GROUPS), :] = acc[g * gp:(g + 1) * gp, lo:lo + LANES]
        o_ref[:, lo:lo + LANES] = h1_ref[:, lo:lo + LANES] + nat_ref[cb]
    h2 = o_ref[...]
    z = _dot(_rms(h2, pg_ref[...], NORM_EPS).astype(BF16), wpg_ref[...])
    gate = 1.0 / (1.0 + jnp.exp(-z))
    e = _dot(p_ref[...].astype(BF16), wpp_ref[...])
    out = h2 + gate * e
    o_ref[...] = _rms(out, fg_ref[...], NORM_EPS) if last_layer else out


def _ffn(hnp, h1, p, wup, cw, cb, wdn, pg, wpg, wpp, final_gain, last_layer):
    b = h1.shape[0]
    ts = FF_TS
    nt = SEQ // ts
    gblk = FF_GP // FF_HALO
    row = lambda w: pl.BlockSpec((None, ts, w), lambda i, j: (i, j, 0))
    full = _const_spec
    halo = lambda imap: pl.BlockSpec((None, None, None, FF_HALO, D_MODEL), imap)
    return pl.pallas_call(
        functools.partial(_ffn_kernel, last_layer=last_layer),
        grid=(b, nt),
        in_specs=[pl.BlockSpec((None, None, FF_GROUPS, FF_GP, D_MODEL), lambda i, j: (i, j, 0, 0, 0)),
                  halo(lambda i, j: (i, jnp.maximum(j - 1, 0), FF_GROUPS - 1, gblk - 1, 0)),
                  halo(lambda i, j: (i, jnp.minimum(j + 1, nt - 1), 0, 0, 0)),
                  row(D_MODEL), row(PLE_DIM),
                  full((D_MODEL, 2 * D_FF)), full((3, 2 * D_FF)), full((1, 2 * D_FF)),
                  full((D_FF, D_MODEL)), full((1, D_MODEL)), full((D_MODEL, D_MODEL)),
                  full((PLE_DIM, D_MODEL)), full((1, D_MODEL))],
        out_specs=row(D_MODEL),
        out_shape=jax.ShapeDtypeStruct((b, SEQ, D_MODEL), F32),
        scratch_shapes=[pltpu.VMEM((_FF_ROWS, D_MODEL), BF16),
                        pltpu.VMEM((2, _FF_ROWS, FF_CHUNK), F32),
                        pltpu.VMEM((2, _FF_ROWS, FF_CHUNK), F32),
                        pltpu.VMEM((ts, D_FF), BF16),
                        pltpu.VMEM((D_MODEL // LANES, ts, LANES), F32)],
        compiler_params=_params(("parallel", "parallel")),
        name="ffn_ple",
    )(hnp, hnp, hnp, h1, p, wup, cw, cb, wdn, pg, wpg, wpp, final_gain)


def _prep_win(w):
    a, bq, bkv, bkr, c, d = jnp.split(w, [768, 1152, 1408, 1440, 2208], axis=1)
    scale = HEAD_DIM ** -0.5
    a = jnp.concatenate([a[:, :256] * scale, a[:, 256:]], axis=1)
    d = jnp.concatenate([d[:, :256] * scale, d[:, 256:]], axis=1)
    z = lambda n: jnp.zeros((w.shape[0], n), w.dtype)
    kr = jnp.concatenate([z(MLA_NOPE_DIM), bkr, z(LANES - MLA_NOPE_DIM - MLA_ROPE_DIM)], axis=1)
    return jnp.concatenate([a, bq, bkv, kr, c, d], axis=1).astype(BF16)


def _prep_wuq(w):
    w = w.reshape(MLA_Q_RANK, 4, MLA_NOPE_DIM + MLA_ROPE_DIM)
    w = jnp.pad(w, ((0, 0), (0, 0), (0, LANES - MLA_NOPE_DIM - MLA_ROPE_DIM)))
    return w.reshape(MLA_Q_RANK, 4 * LANES).astype(BF16)


def _prep_wukv(w):
    w = w.reshape(MLA_KV_RANK, 4, MLA_NOPE_DIM + HEAD_DIM)
    kn = jnp.pad(w[:, :, :MLA_NOPE_DIM], ((0, 0), (0, 0), (0, LANES - MLA_NOPE_DIM)))
    v = w[:, :, MLA_NOPE_DIM:]
    return jnp.concatenate([kn.reshape(MLA_KV_RANK, 4 * LANES),
                            v.reshape(MLA_KV_RANK, 4 * HEAD_DIM)], axis=1).astype(BF16)


def kernel(x, p, attn_norm, w_in, mla_q_norm, w_uq, mla_kv_norm, w_ukv, lam_q1, lam_k1, lam_q2, lam_k2,
           diff_subln, na_rpb, w_o, ffn_norm, w_up, conv_w, conv_b, w_down, ple_norm, w_ple_gate,
           w_ple_proj, final_norm):
    mla_scale = (MLA_NOPE_DIM + MLA_ROPE_DIM) ** -0.5 * LOG2E
    diff_scale = DIFF_QK_DIM ** -0.5 * LOG2E
    tabs = (_rope_tables(HEAD_DIM, (0, 64), 1.0),
            _rope_tables(MLA_ROPE_DIM, (64,), mla_scale),
            _rope_tables(MLA_ROPE_DIM, (64,), 1.0),
            _rope_tables(DIFF_QK_DIM, (0, 32, 64, 96), diff_scale),
            _rope_tables(DIFF_QK_DIM, (0, 32, 64, 96), 1.0))
    masks = _band_masks()
    row = lambda v: v.reshape(1, -1)
    h = x
    for i in range(DEPTH):
        aq, ak, av, bq, bk, bv, cq, ck, cv, dq, dk, dv = _inproj(
            h, row(attn_norm[i]), _prep_win(w_in[i]), row(mla_q_norm[i]), _prep_wuq(w_uq[i]),
            row(mla_kv_norm[i]), _prep_wukv(w_ukv[i]), tabs)
        oa = _band_attention(aq, ak, av, masks)
        ob = _mla_attention(bq, bk, bv)
        lam_init = 0.8 - 0.6 * math.exp(-0.3 * i)
        lam_vecs = jnp.stack([lam_q1[i], lam_k1[i], lam_q2[i], lam_k2[i]])
        oc = _diff_attention(lam_vecs, row(jnp.tile(diff_subln[i], 2)), cq, ck, cv, lam_init)
        od = _na_attention(_na_bias(na_rpb[i]), dq, dk, dv)
        h1, hnp = _oproj(h, oa, ob, oc, od, w_o[i].astype(BF16), row(ffn_norm[i]))
        h = _ffn(hnp, h1, p[i], w_up[i].astype(BF16), conv_w[i], row(conv_b[i]),
                 w_down[i].astype(BF16), row(ple_norm[i]), w_ple_gate[i].astype(BF16),
                 w_ple_proj[i].astype(BF16), row(final_norm), last_layer=i == DEPTH - 1)
    return h
```

```python
import functools
import math

import numpy as np
import jax
import jax.numpy as jnp
from jax import lax
from jax.experimental import pallas as pl
from jax.experimental.pallas import tpu as pltpu

D_MODEL = 1024
SEQ = 2048
DEPTH = 4
GROUP_HEADS = 4
HEAD_DIM = 64
LANES = 128
RADIUS = 64
MLA_Q_RANK = 384
MLA_KV_RANK = 256
MLA_NOPE_DIM = 64
MLA_ROPE_DIM = 32
DIFF_QK_DIM = 32
DIFF_EPS = 1e-5
GRID_W = 64
NA_WIN_ROWS = 8
NA_WIN_COLS = 16
D_FF = 2816
PLE_DIM = 256
ROPE_THETA = 10000.0
NORM_EPS = 1e-6
NEG_INF = -1e30
LOG2E = math.log2(math.e)

F32 = jnp.float32
BF16 = jnp.bfloat16

IN_A = 0
IN_CQ = 768
IN_CKV = 1152
IN_KR = 1408
IN_C = 1536
IN_D = 2304
IN_COLS_P = 3072

VMEM_LIMIT = 56 * 1024 * 1024

IN_TM = 1024
ATT_TQ = 512
MIX_ROWS = 256
BAND_TQ = 512
BAND_TK = BAND_TQ + 2 * RADIUS
DIL = 4
NA_ROWS = 4
NA_KROWS = 12
NA_TABS = 2 * NA_WIN_ROWS
FF_TS = 512
FF_GROUPS = 8
FF_GP = FF_TS // FF_GROUPS
FF_HALO = 16
FF_CHUNK = 256


def _dot(a, b):
    return jnp.dot(a, b, preferred_element_type=F32)


def _dot_nt(a, b):
    return lax.dot_general(a, b, (((1,), (1,)), ((), ())), preferred_element_type=F32)


def _rms(x, gain, eps):
    ms = jnp.mean(x * x, axis=-1, keepdims=True)
    return x * lax.rsqrt(ms + eps) * gain


def _params(sem):
    return pltpu.CompilerParams(dimension_semantics=sem, vmem_limit_bytes=VMEM_LIMIT)


def _const_spec(shape):
    return pl.BlockSpec(shape, lambda *_: (0,) * len(shape), pipeline_mode=pl.Buffered(1))


def _rope_tables(d, groups, scale):
    half = d // 2
    pos = jnp.arange(SEQ, dtype=jnp.int32)
    inv = jnp.power(ROPE_THETA, -jnp.arange(0, d, 2, dtype=F32) / d)
    ang = pos.astype(F32)[:, None] * inv[None, :]
    cos, sin = jnp.cos(ang), jnp.sin(ang)
    idx = np.zeros(LANES, np.int32)
    first = np.zeros(LANES, bool)
    second = np.zeros(LANES, bool)
    for g in groups:
        for j in range(d):
            idx[g + j] = j % half
            (first if j < half else second)[g + j] = True
    is_rope = first | second
    c = jnp.where(is_rope[None, :], cos[:, idx], 1.0) * scale
    sa = jnp.where(first[None, :], -sin[:, idx], 0.0) * scale
    sb = jnp.where(second[None, :], sin[:, idx], 0.0) * scale
    return jnp.stack([c, sa, sb]).astype(F32)


def _rope(x, tab_ref, half):
    return (x * tab_ref[0] + pltpu.roll(x, LANES - half, 1) * tab_ref[1]
            + pltpu.roll(x, half, 1) * tab_ref[2])


def _inproj_kernel(x_ref, g_ref, win_ref, qn_ref, wuq_ref, kvn_ref, wukv_ref,
                   ta_ref, tbq_ref, tbk_ref, tcq_ref, tck_ref,
                   aq_ref, ak_ref, av_ref, bq_ref, bk_ref, bv_ref,
                   cq_ref, ck_ref, cv_ref, dq_ref, dk_ref, dv_ref):
    hn = _rms(x_ref[...], g_ref[...], NORM_EPS).astype(BF16)

    aa = _dot(hn, win_ref[:, IN_A:IN_A + 768])
    for s in range(2):
        lo = s * LANES
        aq_ref[:, lo:lo + LANES] = _rope(aa[:, lo:lo + LANES], ta_ref, 32).astype(BF16)
        ak_ref[:, lo:lo + LANES] = _rope(aa[:, 256 + lo:256 + lo + LANES], ta_ref, 32).astype(BF16)
    av_ref[...] = aa[:, 512:768].astype(BF16)

    bb = _dot(hn, win_ref[:, IN_CQ:IN_C])
    cq = bb[:, 0:MLA_Q_RANK]
    ckv = bb[:, IN_CKV - IN_CQ:IN_KR - IN_CQ]
    qb = _dot(_rms(cq, qn_ref[...], NORM_EPS).astype(BF16), wuq_ref[...])
    kvb = _dot(_rms(ckv, kvn_ref[...], NORM_EPS).astype(BF16), wukv_ref[...])
    kr = _rope(bb[:, IN_KR - IN_CQ:IN_C - IN_CQ], tbk_ref, 16)
    for h in range(4):
        lo = h * LANES
        bq_ref[h] = _rope(qb[:, lo:lo + LANES], tbq_ref, 16).astype(BF16)
        bk_ref[h] = (kvb[:, lo:lo + LANES] + kr).astype(BF16)
    for s in range(2):
        bv_ref[s] = kvb[:, 512 + s * LANES:512 + (s + 1) * LANES].astype(BF16)

    cc = _dot(hn, win_ref[:, IN_C:IN_C + 768])
    for s in range(2):
        lo = s * LANES
        cq_ref[s] = _rope(cc[:, lo:lo + LANES], tcq_ref, 16).astype(BF16)
        ck_ref[s] = _rope(cc[:, 256 + lo:256 + lo + LANES], tck_ref, 16).astype(BF16)
        cv_ref[s] = cc[:, 512 + lo:512 + lo + LANES].astype(BF16)

    dd = _dot(hn, win_ref[:, IN_D:IN_D + 768])
    for s in range(2):
        lo = s * LANES
        dq_ref[s] = dd[:, lo:lo + LANES].astype(BF16)
        dk_ref[s] = dd[:, 256 + lo:256 + lo + LANES].astype(BF16)
        dv_ref[s] = dd[:, 512 + lo:512 + lo + LANES].astype(BF16)


def _inproj(h, g, win, qn, wuq, kvn, wukv, tabs):
    b = h.shape[0]
    tm = IN_TM
    grid = (b, SEQ // tm)
    full = lambda shape: pl.BlockSpec(shape, lambda i, j: (0,) * len(shape))
    tab = pl.BlockSpec((3, tm, LANES), lambda i, j: (0, j, 0))
    nat = pl.BlockSpec((None, tm, 256), lambda i, j: (i, j, 0))
    slab = lambda n: pl.BlockSpec((None, n, tm, LANES), lambda i, j: (i, 0, j, 0))
    nat_s = jax.ShapeDtypeStruct((b, SEQ, 256), BF16)
    slab_s = lambda n: jax.ShapeDtypeStruct((b, n, SEQ, LANES), BF16)
    return pl.pallas_call(
        _inproj_kernel,
        grid=grid,
        in_specs=[pl.BlockSpec((None, tm, D_MODEL), lambda i, j: (i, j, 0)),
                  full((1, D_MODEL)), full((D_MODEL, IN_COLS_P)),
                  full((1, MLA_Q_RANK)), full((MLA_Q_RANK, 512)),
                  full((1, MLA_KV_RANK)), full((MLA_KV_RANK, 768)),
                  tab, tab, tab, tab, tab],
        out_specs=[nat, nat, nat, slab(4), slab(4), slab(2),
                   slab(2), slab(2), slab(2), slab(2), slab(2), slab(2)],
        out_shape=[nat_s, nat_s, nat_s, slab_s(4), slab_s(4), slab_s(2),
                   slab_s(2), slab_s(2), slab_s(2), slab_s(2), slab_s(2), slab_s(2)],
        compiler_params=_params(("parallel", "parallel")),
        name="inproj",
    )(h, g, win, qn, wuq, kvn, wukv, *tabs)


def _band_masks():
    length = SEQ // DIL
    neg = lambda ok: jnp.where(ok, 0.0, NEG_INF).astype(F32)
    d1 = (jnp.arange(BAND_TK)[None, :] - jnp.arange(BAND_TQ)[:, None])[None] \
        - RADIUS * jnp.arange(3)[:, None, None]
    d = jnp.arange(length)[None, :] - jnp.arange(length)[:, None]
    step = 16 // DIL
    return (neg(jnp.abs(d1) <= RADIUS), neg(jnp.abs(d) <= RADIUS),
            neg((d % step == 0) & (jnp.abs(d) <= RADIUS * step)))


def _softmax_pv(z, v):
    m = jnp.max(z, axis=-1, keepdims=True)
    p = jnp.exp(z - m)
    den = jnp.sum(p, axis=-1, keepdims=True)
    return _dot(p.astype(BF16), v) / den, m + jnp.log(den)


def _band_kernel(q_ref, k_ref, v_ref, m1_ref, m4_ref, m16_ref, o_ref,
                 xf_ref, qr_ref, kr_ref, vr_ref, os_ref, ls_ref):
    length = SEQ // DIL
    lane = lax.broadcasted_iota(jnp.int32, (BAND_TQ, LANES), 1)
    head0 = lane < HEAD_DIM

    for src, dst in ((q_ref, qr_ref), (k_ref, kr_ref), (v_ref, vr_ref)):
        for s in range(2):
            xf_ref[...] = src[:, s * LANES:(s + 1) * LANES].astype(F32)
            for r in range(DIL):
                dst[r, s] = xf_ref[pl.ds(r, length, stride=DIL), :].astype(BF16)

    def head_pair(q, k, v, masks):
        scores = []
        for t in range(2):
            qm = jnp.where(head0 if t == 0 else ~head0, q, jnp.zeros_like(q))
            scores.append(_dot_nt(qm, k))
        res = []
        for mask in masks:
            r0, r1 = _softmax_pv(scores[0] + mask, v), _softmax_pv(scores[1] + mask, v)
            res.append((jnp.where(head0, r0[0], r1[0]), jnp.where(head0, r0[1], r1[1])))
        return res

    def dil1_block(i, carry):
        q0 = pl.multiple_of(i * BAND_TQ, BAND_TQ)
        ws = pl.multiple_of(jnp.clip(q0 - RADIUS, 0, SEQ - BAND_TK), RADIUS)
        mask = m1_ref[(q0 - ws) // RADIUS]
        for s in range(2):
            lo = s * LANES
            (o, lse), = head_pair(q_ref[pl.ds(q0, BAND_TQ), lo:lo + LANES],
                                  k_ref[pl.ds(ws, BAND_TK), lo:lo + LANES],
                                  v_ref[pl.ds(ws, BAND_TK), lo:lo + LANES], (mask,))
            os_ref[0, s, pl.ds(q0, BAND_TQ), :] = o
            ls_ref[0, s, pl.ds(q0, BAND_TQ), :] = lse
        return carry

    lax.fori_loop(0, SEQ // BAND_TQ, dil1_block, 0)

    def residue_block(r, carry):
        rows = pl.ds(r, length, stride=DIL)
        for s in range(2):
            res = head_pair(qr_ref[r, s], kr_ref[r, s], vr_ref[r, s], (m4_ref[...], m16_ref[...]))
            for pat in range(2):
                os_ref[1 + pat, s, rows, :] = res[pat][0]
                ls_ref[1 + pat, s, rows, :] = res[pat][1]
        return carry

    lax.fori_loop(0, DIL, residue_block, 0)

    def mix_block(i, carry):
        rows = pl.ds(pl.multiple_of(i * MIX_ROWS, MIX_ROWS), MIX_ROWS)
        for s in range(2):
            l1, l4, l16 = ls_ref[0, s, rows, :], ls_ref[1, s, rows, :], ls_ref[2, s, rows, :]
            m = jnp.maximum(jnp.maximum(l1, l4), l16)
            e1, e4, e16 = jnp.exp(l1 - m), jnp.exp(l4 - m), jnp.exp(l16 - m)
            den = e1 + e4 + e16
            o_ref[rows, s * LANES:(s + 1) * LANES] = (
                (e1 / den) * os_ref[0, s, rows, :] + (e4 / den) * os_ref[1, s, rows, :]
                + (e16 / den) * os_ref[2, s, rows, :]).astype(BF16)
        return carry

    lax.fori_loop(0, SEQ // MIX_ROWS, mix_block, 0)


def _band_attention(q, k, v, masks):
    b = q.shape[0]
    length = SEQ // DIL
    spec = pl.BlockSpec((None, SEQ, 256), lambda i: (i, 0, 0))
    return pl.pallas_call(
        _band_kernel,
        grid=(b,),
        in_specs=[spec, spec, spec, _const_spec((3, BAND_TQ, BAND_TK)),
                  _const_spec((length, length)), _const_spec((length, length))],
        out_specs=spec,
        out_shape=jax.ShapeDtypeStruct((b, SEQ, 256), BF16),
        scratch_shapes=[pltpu.VMEM((SEQ, LANES), F32),
                        pltpu.VMEM((DIL, 2, length, LANES), BF16),
                        pltpu.VMEM((DIL, 2, length, LANES), BF16),
                        pltpu.VMEM((DIL, 2, length, LANES), BF16),
                        pltpu.VMEM((3, 2, SEQ, LANES), F32),
                        pltpu.VMEM((3, 2, SEQ, LANES), F32)],
        compiler_params=_params(("parallel",)),
        name="band_mix",
    )(q, k, v, *masks)


def _pipelined_blocks(nblk, scores, consume, s0_ref, s1_ref):
    scores(0, s0_ref)

    def body(j, carry):
        b0 = 2 * j
        scores(b0 + 1, s1_ref)
        consume(b0, s0_ref)
        scores(b0 + 2, s0_ref)
        consume(b0 + 1, s1_ref)
        return carry

    lax.fori_loop(0, nblk // 2 - 1, body, 0)
    scores(nblk - 1, s1_ref)
    consume(nblk - 2, s0_ref)
    consume(nblk - 1, s1_ref)


def _block_rows(blk):
    start = blk * ATT_TQ
    if not isinstance(start, int):
        start = pl.multiple_of(start, ATT_TQ)
    return pl.ds(start, ATT_TQ)


def _mla_kernel(q_ref, k_ref, v_ref, o_ref, s0_ref, s1_ref):
    tq = ATT_TQ

    def head_pair(s, carry):
        vt = v_ref[s].astype(F32).T.astype(BF16)
        sub0 = lax.broadcasted_iota(jnp.int32, (LANES, tq), 0) < HEAD_DIM

        def scores(blk, s_ref):
            rows = _block_rows(blk)
            for t in range(2):
                s_ref[t] = _dot_nt(k_ref[2 * s + t], q_ref[2 * s + t, rows, :])

        def consume(blk, s_ref):
            rows = _block_rows(blk)
            outs = []
            for t in range(2):
                sc = s_ref[t]
                p = jnp.exp2(sc - jnp.max(sc, axis=0, keepdims=True))
                den = jnp.sum(p, axis=0, keepdims=True)
                outs.append(_dot(vt, p.astype(BF16)) / den)
            o_ref[s, rows, :] = jnp.where(sub0, outs[0], outs[1]).T.astype(BF16)

        _pipelined_blocks(SEQ // tq, scores, consume, s0_ref, s1_ref)
        return carry

    lax.fori_loop(0, 2, head_pair, 0)


def _mla_attention(q, k, v):
    b = q.shape[0]
    return pl.pallas_call(
        _mla_kernel,
        grid=(b,),
        in_specs=[pl.BlockSpec((None, 4, SEQ, LANES), lambda i: (i, 0, 0, 0)),
                  pl.BlockSpec((None, 4, SEQ, LANES), lambda i: (i, 0, 0, 0)),
                  pl.BlockSpec((None, 2, SEQ, LANES), lambda i: (i, 0, 0, 0))],
        out_specs=pl.BlockSpec((None, 2, SEQ, LANES), lambda i: (i, 0, 0, 0)),
        out_shape=jax.ShapeDtypeStruct((b, 2, SEQ, LANES), BF16),
        scratch_shapes=[pltpu.VMEM((2, SEQ, ATT_TQ), F32), pltpu.VMEM((2, SEQ, ATT_TQ), F32)],
        compiler_params=_params(("parallel",)),
        name="mla_attn",
    )(q, k, v)


def _diff_kernel(lam_ref, sub_ref, q_ref, k_ref, v_ref, o_ref, s0_ref, s1_ref, *, lam_init):
    tq = ATT_TQ
    lv = lam_ref[...]
    lam = (jnp.exp(jnp.sum(lv[0:1] * lv[1:2], axis=-1, keepdims=True))
           - jnp.exp(jnp.sum(lv[2:3] * lv[3:4], axis=-1, keepdims=True)) + lam_init)
    lane = lax.broadcasted_iota(jnp.int32, (tq, LANES), 1)
    head0 = lane < HEAD_DIM

    def head_pair(s, carry):
        vt = v_ref[s].astype(F32).T.astype(BF16)
        sub0 = lax.broadcasted_iota(jnp.int32, (LANES, tq), 0) < HEAD_DIM

        def scores(blk, s_ref):
            q = q_ref[s, _block_rows(blk), :]
            for u in range(4):
                lo = u * DIFF_QK_DIM
                qm = jnp.where((lane >= lo) & (lane < lo + DIFF_QK_DIM), q, jnp.zeros_like(q))
                s_ref[u] = _dot_nt(k_ref[s], qm)

        def consume(blk, s_ref):
            rows = _block_rows(blk)
            outs = []
            for t in range(2):
                s1, s2 = s_ref[2 * t], s_ref[2 * t + 1]
                p1 = jnp.exp2(s1 - jnp.max(s1, axis=0, keepdims=True))
                p2 = jnp.exp2(s2 - jnp.max(s2, axis=0, keepdims=True))
                d1 = jnp.sum(p1, axis=0, keepdims=True)
                d2 = jnp.sum(p2, axis=0, keepdims=True)
                pn = p1 - (lam * d1 / d2) * p2
                outs.append(_dot(vt, pn.astype(BF16)) / d1)
            o = jnp.where(sub0, outs[0], outs[1]).T
            o2 = o * o
            ss0 = jnp.sum(jnp.where(head0, o2, 0.0), axis=-1, keepdims=True)
            ss1 = jnp.sum(jnp.where(head0, 0.0, o2), axis=-1, keepdims=True)
            ms = jnp.where(head0, ss0, ss1) * (1.0 / HEAD_DIM)
            y = o * lax.rsqrt(ms + DIFF_EPS) * sub_ref[...]
            o_ref[s, rows, :] = (y * (1.0 - lam_init)).astype(BF16)

        _pipelined_blocks(SEQ // tq, scores, consume, s0_ref, s1_ref)
        return carry

    lax.fori_loop(0, 2, head_pair, 0)


def _diff_attention(lam_vecs, subln2, q, k, v, lam_init):
    b = q.shape[0]
    spec = pl.BlockSpec((None, 2, SEQ, LANES), lambda i: (i, 0, 0, 0))
    return pl.pallas_call(
        functools.partial(_diff_kernel, lam_init=lam_init),
        grid=(b,),
        in_specs=[pl.BlockSpec((4, DIFF_QK_DIM), lambda i: (0, 0)),
                  pl.BlockSpec((1, LANES), lambda i: (0, 0)),
                  spec, spec, spec],
        out_specs=spec,
        out_shape=jax.ShapeDtypeStruct((b, 2, SEQ, LANES), BF16),
        scratch_shapes=[pltpu.VMEM((4, SEQ, ATT_TQ), F32), pltpu.VMEM((4, SEQ, ATT_TQ), F32)],
        compiler_params=_params(("parallel",)),
        name="diff_attn",
    )(lam_vecs, subln2, q, k, v)


def _na_bias_kernel(rpb_ref, o_ref):
    nr = 2 * NA_WIN_ROWS - 1
    nc = 2 * NA_WIN_COLS - 1
    qc = lax.broadcasted_iota(jnp.int32, (GRID_W, LANES), 0)
    lane = lax.broadcasted_iota(jnp.int32, (GRID_W, LANES), 1)
    kc = lane % GRID_W
    left = lane < GRID_W
    start = jnp.clip(qc - NA_WIN_COLS // 2, 0, GRID_W - NA_WIN_COLS)
    col_ok = (kc >= start) & (kc < start + NA_WIN_COLS)
    idx_c = jnp.clip(kc - qc, -(NA_WIN_COLS - 1), NA_WIN_COLS - 1) + (NA_WIN_COLS - 1)
    sel = [idx_c == t for t in range(nc)]

    def body(i, carry):
        h = i // NA_TABS
        d = i % NA_TABS - NA_WIN_ROWS
        acc = jnp.zeros((GRID_W, LANES), F32)
        r_lo = jnp.clip(d + NA_WIN_ROWS - 1, 0, nr - 1)
        r_hi = jnp.clip(d + NA_WIN_ROWS, 0, nr - 1)
        for t in range(nc):
            v_lo = rpb_ref[(h * nr + r_lo) * nc + t]
            v_hi = rpb_ref[(h * nr + r_hi) * nc + t]
            acc = acc + jnp.where(sel[t], jnp.where(left, v_lo, v_hi), 0.0)
        dr = jnp.where(left, d, d + 1)
        ok = col_ok & (dr >= -(NA_WIN_ROWS - 1)) & (dr <= NA_WIN_ROWS - 1)
        o_ref[i] = jnp.where(ok, acc, NEG_INF)
        return carry

    lax.fori_loop(0, GROUP_HEADS * NA_TABS, body, 0)


def _na_bias(rpb):
    return pl.pallas_call(
        _na_bias_kernel,
        in_specs=[pl.BlockSpec(memory_space=pltpu.SMEM)],
        out_specs=pl.BlockSpec(memory_space=pltpu.VMEM),
        out_shape=jax.ShapeDtypeStruct((GROUP_HEADS * NA_TABS, GRID_W, LANES), F32),
        name="na_bias",
    )(rpb.reshape(-1))


def _na_kernel(bias_ref, q_ref, k_ref, v_ref, o_ref):
    rows = SEQ // GRID_W
    tq = NA_ROWS * GRID_W
    tk = NA_KROWS * GRID_W
    lane = lax.broadcasted_iota(jnp.int32, (tq, LANES), 1)
    head0 = lane < HEAD_DIM
    left = lax.broadcasted_iota(jnp.int32, (GRID_W, LANES), 1) < GRID_W
    neg = jnp.full((GRID_W, LANES), NEG_INF, F32)

    def head_pair(pair, carry):
        for blk in range(rows // NA_ROWS):
            r0 = blk * NA_ROWS
            ws = min(max(r0 - NA_WIN_ROWS // 2, 0), rows - NA_KROWS)
            q = q_ref[pair, r0 * GRID_W:r0 * GRID_W + tq, :]
            kw = k_ref[pair, ws * GRID_W:ws * GRID_W + tk, :]
            vw = v_ref[pair, ws * GRID_W:ws * GRID_W + tk, :]
            outs = []
            for t in range(2):
                tab0 = (2 * pair + t) * NA_TABS
                rows_bias = []
                for ri in range(NA_ROWS):
                    r = r0 + ri
                    lo_r = min(max(r - NA_WIN_ROWS // 2, 0), rows - NA_WIN_ROWS) - r
                    pieces = []
                    for j2 in range(NA_KROWS // 2):
                        d = ws + 2 * j2 - r
                        ok_l = lo_r <= d <= lo_r + NA_WIN_ROWS - 1
                        ok_r = lo_r <= d + 1 <= lo_r + NA_WIN_ROWS - 1
                        if not (ok_l or ok_r):
                            pieces.append(neg)
                            continue
                        tab = bias_ref[tab0 + d + NA_WIN_ROWS]
                        if not ok_l:
                            tab = jnp.where(left, NEG_INF, tab)
                        elif not ok_r:
                            tab = jnp.where(left, tab, NEG_INF)
                        pieces.append(tab)
                    rows_bias.append(jnp.concatenate(pieces, axis=1))
                bias = jnp.concatenate(rows_bias, axis=0)
                qm = jnp.where(head0 if t == 0 else ~head0, q, jnp.zeros_like(q))
                sc = _dot_nt(qm, kw) + bias
                m = jnp.max(sc, axis=-1, keepdims=True)
                p = jnp.exp(sc - m)
                den = jnp.sum(p, axis=-1, keepdims=True)
                outs.append(_dot(p.astype(BF16), vw) / den)
            o_ref[pair, r0 * GRID_W:r0 * GRID_W + tq, :] = (
                jnp.where(head0, outs[0], outs[1]).astype(BF16))
        return carry

    lax.fori_loop(0, 2, head_pair, 0)


def _na_attention(bias, q, k, v):
    b = q.shape[0]
    spec = pl.BlockSpec((None, 2, SEQ, LANES), lambda i: (i, 0, 0, 0))
    return pl.pallas_call(
        _na_kernel,
        grid=(b,),
        in_specs=[_const_spec((GROUP_HEADS * NA_TABS, GRID_W, LANES)), spec, spec, spec],
        out_specs=spec,
        out_shape=jax.ShapeDtypeStruct((b, 2, SEQ, LANES), BF16),
        compiler_params=_params(("parallel",)),
        name="na_attn",
    )(bias, q, k, v)


def _oproj_kernel(h_ref, oa_ref, ob_ref, oc_ref, od_ref, wo_ref, g_ref,
                  h1_ref, hnp_ref, mix_ref, hs_ref):
    mix_ref[:, 0:256] = oa_ref[...]
    mix_ref[:, 256:384] = ob_ref[0]
    mix_ref[:, 384:512] = ob_ref[1]
    mix_ref[:, 512:640] = oc_ref[0]
    mix_ref[:, 640:768] = oc_ref[1]
    mix_ref[:, 768:896] = od_ref[0]
    mix_ref[:, 896:1024] = od_ref[1]
    h1 = h_ref[...] + _dot(mix_ref[...], wo_ref[...])
    h1_ref[...] = h1
    hn = _rms(h1, g_ref[...], NORM_EPS)
    for cb in range(D_MODEL // LANES):
        lo = cb * LANES
        hs_ref[cb] = hn[:, lo:lo + LANES]
        for g in range(FF_GROUPS):
            hnp_ref[g, :, lo:lo + LANES] = hs_ref[cb, pl.ds(g, FF_GP, stride=FF_GROUPS), :].astype(BF16)


def _oproj(h, oa, ob, oc, od, wo, g):
    b = h.shape[0]
    tm = FF_TS
    nt = SEQ // tm
    row = lambda w: pl.BlockSpec((None, tm, w), lambda i, j: (i, j, 0))
    slab = pl.BlockSpec((None, 2, tm, LANES), lambda i, j: (i, 0, j, 0))
    return pl.pallas_call(
        _oproj_kernel,
        grid=(b, nt),
        in_specs=[row(D_MODEL), row(256), slab, slab, slab,
                  _const_spec((D_MODEL, D_MODEL)), _const_spec((1, D_MODEL))],
        out_specs=[row(D_MODEL),
                   pl.BlockSpec((None, None, FF_GROUPS, FF_GP, D_MODEL), lambda i, j: (i, j, 0, 0, 0))],
        out_shape=[jax.ShapeDtypeStruct((b, SEQ, D_MODEL), F32),
                   jax.ShapeDtypeStruct((b, nt, FF_GROUPS, FF_GP, D_MODEL), BF16)],
        scratch_shapes=[pltpu.VMEM((tm, D_MODEL), BF16), pltpu.VMEM((D_MODEL // LANES, tm, LANES), F32)],
        compiler_params=_params(("parallel", "parallel")),
        name="oproj",
    )(h, oa, ob, oc, od, wo, g)


def _gelu_tanh(x):
    c = math.sqrt(2.0 / math.pi)
    return x * (0.5 * (1.0 + jnp.tanh(c * (x + 0.044715 * (x * x * x)))))


_FF_OFF = {7: FF_HALO, 0: FF_HALO + 7 * FF_GP}
_FF_OFF.update({g: FF_HALO + g * FF_GP for g in range(1, 7)})
_FF_ROWS = FF_TS + 2 * FF_HALO


def _ffn_kernel(hnp_ref, prev_ref, next_ref, h1_ref, p_ref, wup_ref, cw_ref, cb_ref, wdn_ref,
                pg_ref, wpg_ref, wpp_ref, fg_ref, o_ref, ext_ref, ug_ref, uv_ref, a_ref, nat_ref,
                *, last_layer):
    t = pl.program_id(1)
    gp = FF_GP
    prev = prev_ref[...]
    nxt = next_ref[...]
    ext_ref[0:FF_HALO, :] = jnp.where(t > 0, prev, jnp.zeros_like(prev))
    ext_ref[_FF_ROWS - FF_HALO:, :] = jnp.where(t < pl.num_programs(1) - 1, nxt, jnp.zeros_like(nxt))
    for g in range(FF_GROUPS):
        ext_ref[_FF_OFF[g]:_FF_OFF[g] + gp, :] = hnp_ref[g]
    ext = ext_ref[...]

    def window(u_ref, slot, g, shift):
        if shift < 0 and g == 0:
            lo = _FF_OFF[7] - 1
        elif shift > 0 and g == FF_GROUPS - 1:
            lo = _FF_OFF[0] + 1
        else:
            lo = _FF_OFF[g + shift]
        return u_ref[slot, lo:lo + gp, :]

    def conv(u_ref, slot, g, col):
        w = cw_ref[:, col:col + FF_CHUNK]
        return (window(u_ref, slot, g, -1) * w[0:1] + window(u_ref, slot, g, 0) * w[1:2]
                + window(u_ref, slot, g, 1) * w[2:3] + cb_ref[:, col:col + FF_CHUNK])

    for c in range(D_FF // FF_CHUNK):
        gcol = c * FF_CHUNK
        vcol = D_FF + c * FF_CHUNK
        slot = c % 2
        ug_ref[slot] = _dot(ext, wup_ref[:, gcol:gcol + FF_CHUNK])
        uv_ref[slot] = _dot(ext, wup_ref[:, vcol:vcol + FF_CHUNK])
        for g in range(FF_GROUPS):
            a_ref[g * gp:(g + 1) * gp, gcol:gcol + FF_CHUNK] = (
                _gelu_tanh(conv(ug_ref, slot, g, gcol)) * conv(uv_ref, slot, g, vcol)).astype(BF16)

    acc = _dot(a_ref[...], wdn_ref[...])
    for cb in range(D_MODEL // LANES):
        lo = cb * LANES
        for g in range(FF_GROUPS):
            nat_ref[cb, pl.ds(g, gp, stride=FF_GROUPS), :] = acc[g * gp:(g + 1) * gp, lo:lo + LANES]
        o_ref[:, lo:lo + LANES] = h1_ref[:, lo:lo + LANES] + nat_ref[cb]
    h2 = o_ref[...]
    z = _dot(_rms(h2, pg_ref[...], NORM_EPS).astype(BF16), wpg_ref[...])
    gate = 1.0 / (1.0 + jnp.exp(-z))
    e = _dot(p_ref[...].astype(BF16), wpp_ref[...])
    out = h2 + gate * e
    o_ref[...] = _rms(out, fg_ref[...], NORM_EPS) if last_layer else out


def _ffn(hnp, h1, p, wup, cw, cb, wdn, pg, wpg, wpp, final_gain, last_layer):
    b = h1.shape[0]
    ts = FF_TS
    nt = SEQ // ts
    gblk = FF_GP // FF_HALO
    row = lambda w: pl.BlockSpec((None, ts, w), lambda i, j: (i, j, 0))
    full = _const_spec
    halo = lambda imap: pl.BlockSpec((None, None, None, FF_HALO, D_MODEL), imap)
    return pl.pallas_call(
        functools.partial(_ffn_kernel, last_layer=last_layer),
        grid=(b, nt),
        in_specs=[pl.BlockSpec((None, None, FF_GROUPS, FF_GP, D_MODEL), lambda i, j: (i, j, 0, 0, 0)),
                  halo(lambda i, j: (i, jnp.maximum(j - 1, 0), FF_GROUPS - 1, gblk - 1, 0)),
                  halo(lambda i, j: (i, jnp.minimum(j + 1, nt - 1), 0, 0, 0)),
                  row(D_MODEL), row(PLE_DIM),
                  full((D_MODEL, 2 * D_FF)), full((3, 2 * D_FF)), full((1, 2 * D_FF)),
                  full((D_FF, D_MODEL)), full((1, D_MODEL)), full((D_MODEL, D_MODEL)),
                  full((PLE_DIM, D_MODEL)), full((1, D_MODEL))],
        out_specs=row(D_MODEL),
        out_shape=jax.ShapeDtypeStruct((b, SEQ, D_MODEL), F32),
        scratch_shapes=[pltpu.VMEM((_FF_ROWS, D_MODEL), BF16),
                        pltpu.VMEM((2, _FF_ROWS, FF_CHUNK), F32),
                        pltpu.VMEM((2, _FF_ROWS, FF_CHUNK), F32),
                        pltpu.VMEM((ts, D_FF), BF16),
                        pltpu.VMEM((D_MODEL // LANES, ts, LANES), F32)],
        compiler_params=_params(("parallel", "parallel")),
        name="ffn_ple",
    )(hnp, hnp, hnp, h1, p, wup, cw, cb, wdn, pg, wpg, wpp, final_gain)


def _prep_win(w):
    a, bq, bkv, bkr, c, d = jnp.split(w, [768, 1152, 1408, 1440, 2208], axis=1)
    scale = HEAD_DIM ** -0.5
    a = jnp.concatenate([a[:, :256] * scale, a[:, 256:]], axis=1)
    d = jnp.concatenate([d[:, :256] * scale, d[:, 256:]], axis=1)
    z = lambda n: jnp.zeros((w.shape[0], n), w.dtype)
    kr = jnp.concatenate([z(MLA_NOPE_DIM), bkr, z(LANES - MLA_NOPE_DIM - MLA_ROPE_DIM)], axis=1)
    return jnp.concatenate([a, bq, bkv, kr, c, d], axis=1).astype(BF16)


def _prep_wuq(w):
    w = w.reshape(MLA_Q_RANK, 4, MLA_NOPE_DIM + MLA_ROPE_DIM)
    w = jnp.pad(w, ((0, 0), (0, 0), (0, LANES - MLA_NOPE_DIM - MLA_ROPE_DIM)))
    return w.reshape(MLA_Q_RANK, 4 * LANES).astype(BF16)


def _prep_wukv(w):
    w = w.reshape(MLA_KV_RANK, 4, MLA_NOPE_DIM + HEAD_DIM)
    kn = jnp.pad(w[:, :, :MLA_NOPE_DIM], ((0, 0), (0, 0), (0, LANES - MLA_NOPE_DIM)))
    v = w[:, :, MLA_NOPE_DIM:]
    return jnp.concatenate([kn.reshape(MLA_KV_RANK, 4 * LANES),
                            v.reshape(MLA_KV_RANK, 4 * HEAD_DIM)], axis=1).astype(BF16)


def kernel(x, p, attn_norm, w_in, mla_q_norm, w_uq, mla_kv_norm, w_ukv, lam_q1, lam_k1, lam_q2, lam_k2,
           diff_subln, na_rpb, w_o, ffn_norm, w_up, conv_w, conv_b, w_down, ple_norm, w_ple_gate,
           w_ple_proj, final_norm):
    mla_scale = (MLA_NOPE_DIM + MLA_ROPE_DIM) ** -0.5 * LOG2E
    diff_scale = DIFF_QK_DIM ** -0.5 * LOG2E
    tabs = (_rope_tables(HEAD_DIM, (0, 64), 1.0),
            _rope_tables(MLA_ROPE_DIM, (64,), mla_scale),
            _rope_tables(MLA_ROPE_DIM, (64,), 1.0),
            _rope_tables(DIFF_QK_DIM, (0, 32, 64, 96), diff_scale),
            _rope_tables(DIFF_QK_DIM, (0, 32, 64, 96), 1.0))
    masks = _band_masks()
    row = lambda v: v.reshape(1, -1)
    h = x
    for i in range(DEPTH):
        aq, ak, av, bq, bk, bv, cq, ck, cv, dq, dk, dv = _inproj(
            h, row(attn_norm[i]), _prep_win(w_in[i]), row(mla_q_norm[i]), _prep_wuq(w_uq[i]),
            row(mla_kv_norm[i]), _prep_wukv(w_ukv[i]), tabs)
        oa = _band_attention(aq, ak, av, masks)
        ob = _mla_attention(bq, bk, bv)
        lam_init = 0.8 - 0.6 * math.exp(-0.3 * i)
        lam_vecs = jnp.stack([lam_q1[i], lam_k1[i], lam_q2[i], lam_k2[i]])
        oc = _diff_attention(lam_vecs, row(jnp.tile(diff_subln[i], 2)), cq, ck, cv, lam_init)
        od = _na_attention(_na_bias(na_rpb[i]), dq, dk, dv)
        h1, hnp = _oproj(h, oa, ob, oc, od, w_o[i].astype(BF16), row(ffn_norm[i]))
        h = _ffn(hnp, h1, p[i], w_up[i].astype(BF16), conv_w[i], row(conv_b[i]),
                 w_down[i].astype(BF16), row(ple_norm[i]), w_ple_gate[i].astype(BF16),
                 w_ple_proj[i].astype(BF16), row(final_norm), last_layer=i == DEPTH - 1)
    return h
```

```python
import functools
import math

import numpy as np
import jax
import jax.numpy as jnp
from jax import lax
from jax.experimental import pallas as pl
from jax.experimental.pallas import tpu as pltpu

D_MODEL = 1024
SEQ = 2048
DEPTH = 4
GROUP_HEADS = 4
HEAD_DIM = 64
LANES = 128
RADIUS = 64
MLA_Q_RANK = 384
MLA_KV_RANK = 256
MLA_NOPE_DIM = 64
MLA_ROPE_DIM = 32
DIFF_QK_DIM = 32
DIFF_EPS = 1e-5
GRID_W = 64
NA_WIN_ROWS = 8
NA_WIN_COLS = 16
D_FF = 2816
PLE_DIM = 256
ROPE_THETA = 10000.0
NORM_EPS = 1e-6
NEG_INF = -1e30
LOG2E = math.log2(math.e)

F32 = jnp.float32
BF16 = jnp.bfloat16

IN_A = 0
IN_CQ = 768
IN_CKV = 1152
IN_KR = 1408
IN_C = 1536
IN_D = 2304
IN_COLS_P = 3072

VMEM_LIMIT = 56 * 1024 * 1024

IN_TM = 1024
ATT_TQ = 512
MIX_ROWS = 256
BAND_TQ = 512
BAND_TK = BAND_TQ + 2 * RADIUS
DIL = 4
NA_ROWS = 4
NA_KROWS = 12
NA_TABS = 2 * NA_WIN_ROWS
FF_TS = 512
FF_GROUPS = 8
FF_GP = FF_TS // FF_GROUPS
FF_HALO = 16
FF_CHUNK = 256


def _dot(a, b):
    return jnp.dot(a, b, preferred_element_type=F32)


def _dot_nt(a, b):
    return lax.dot_general(a, b, (((1,), (1,)), ((), ())), preferred_element_type=F32)


def _rms(x, gain, eps):
    ms = jnp.mean(x * x, axis=-1, keepdims=True)
    return x * lax.rsqrt(ms + eps) * gain


def _params(sem):
    return pltpu.CompilerParams(dimension_semantics=sem, vmem_limit_bytes=VMEM_LIMIT)


def _const_spec(shape):
    return pl.BlockSpec(shape, lambda *_: (0,) * len(shape), pipeline_mode=pl.Buffered(1))


def _rope_tables(d, groups, scale):
    half = d // 2
    pos = jnp.arange(SEQ, dtype=jnp.int32)
    inv = jnp.power(ROPE_THETA, -jnp.arange(0, d, 2, dtype=F32) / d)
    ang = pos.astype(F32)[:, None] * inv[None, :]
    cos, sin = jnp.cos(ang), jnp.sin(ang)
    idx = np.zeros(LANES, np.int32)
    first = np.zeros(LANES, bool)
    second = np.zeros(LANES, bool)
    for g in groups:
        for j in range(d):
            idx[g + j] = j % half
            (first if j < half else second)[g + j] = True
    is_rope = first | second
    c = jnp.where(is_rope[None, :], cos[:, idx], 1.0) * scale
    sa = jnp.where(first[None, :], -sin[:, idx], 0.0) * scale
    sb = jnp.where(second[None, :], sin[:, idx], 0.0) * scale
    return jnp.stack([c, sa, sb]).astype(F32)


def _rope(x, tab_ref, half):
    return (x * tab_ref[0] + pltpu.roll(x, LANES - half, 1) * tab_ref[1]
            + pltpu.roll(x, half, 1) * tab_ref[2])


def _inproj_kernel(x_ref, g_ref, win_ref, qn_ref, wuq_ref, kvn_ref, wukv_ref,
                   ta_ref, tbq_ref, tbk_ref, tcq_ref, tck_ref,
                   aq_ref, ak_ref, av_ref, bq_ref, bk_ref, bv_ref,
                   cq_ref, ck_ref, cv_ref, dq_ref, dk_ref, dv_ref):
    hn = _rms(x_ref[...], g_ref[...], NORM_EPS).astype(BF16)

    aa = _dot(hn, win_ref[:, IN_A:IN_A + 768])
    for s in range(2):
        lo = s * LANES
        aq_ref[:, lo:lo + LANES] = _rope(aa[:, lo:lo + LANES], ta_ref, 32).astype(BF16)
        ak_ref[:, lo:lo + LANES] = _rope(aa[:, 256 + lo:256 + lo + LANES], ta_ref, 32).astype(BF16)
    av_ref[...] = aa[:, 512:768].astype(BF16)

    bb = _dot(hn, win_ref[:, IN_CQ:IN_C])
    cq = bb[:, 0:MLA_Q_RANK]
    ckv = bb[:, IN_CKV - IN_CQ:IN_KR - IN_CQ]
    qb = _dot(_rms(cq, qn_ref[...], NORM_EPS).astype(BF16), wuq_ref[...])
    kvb = _dot(_rms(ckv, kvn_ref[...], NORM_EPS).astype(BF16), wukv_ref[...])
    kr = _rope(bb[:, IN_KR - IN_CQ:IN_C - IN_CQ], tbk_ref, 16)
    for h in range(4):
        lo = h * LANES
        bq_ref[h] = _rope(qb[:, lo:lo + LANES], tbq_ref, 16).astype(BF16)
        bk_ref[h] = (kvb[:, lo:lo + LANES] + kr).astype(BF16)
    for s in range(2):
        bv_ref[s] = kvb[:, 512 + s * LANES:512 + (s + 1) * LANES].astype(BF16)

    cc = _dot(hn, win_ref[:, IN_C:IN_C + 768])
    for s in range(2):
        lo = s * LANES
        cq_ref[s] = _rope(cc[:, lo:lo + LANES], tcq_ref, 16).astype(BF16)
        ck_ref[s] = _rope(cc[:, 256 + lo:256 + lo + LANES], tck_ref, 16).astype(BF16)
        cv_ref[s] = cc[:, 512 + lo:512 + lo + LANES].astype(BF16)

    dd = _dot(hn, win_ref[:, IN_D:IN_D + 768])
    for s in range(2):
        lo = s * LANES
        dq_ref[s] = dd[:, lo:lo + LANES].astype(BF16)
        dk_ref[s] = dd[:, 256 + lo:256 + lo + LANES].astype(BF16)
        dv_ref[s] = dd[:, 512 + lo:512 + lo + LANES].astype(BF16)


def _inproj(h, g, win, qn, wuq, kvn, wukv, tabs):
    b = h.shape[0]
    tm = IN_TM
    grid = (b, SEQ // tm)
    full = lambda shape: pl.BlockSpec(shape, lambda i, j: (0,) * len(shape))
    tab = pl.BlockSpec((3, tm, LANES), lambda i, j: (0, j, 0))
    nat = pl.BlockSpec((None, tm, 256), lambda i, j: (i, j, 0))
    slab = lambda n: pl.BlockSpec((None, n, tm, LANES), lambda i, j: (i, 0, j, 0))
    nat_s = jax.ShapeDtypeStruct((b, SEQ, 256), BF16)
    slab_s = lambda n: jax.ShapeDtypeStruct((b, n, SEQ, LANES), BF16)
    return pl.pallas_call(
        _inproj_kernel,
        grid=grid,
        in_specs=[pl.BlockSpec((None, tm, D_MODEL), lambda i, j: (i, j, 0)),
                  full((1, D_MODEL)), full((D_MODEL, IN_COLS_P)),
                  full((1, MLA_Q_RANK)), full((MLA_Q_RANK, 512)),
                  full((1, MLA_KV_RANK)), full((MLA_KV_RANK, 768)),
                  tab, tab, tab, tab, tab],
        out_specs=[nat, nat, nat, slab(4), slab(4), slab(2),
                   slab(2), slab(2), slab(2), slab(2), slab(2), slab(2)],
        out_shape=[nat_s, nat_s, nat_s, slab_s(4), slab_s(4), slab_s(2),
                   slab_s(2), slab_s(2), slab_s(2), slab_s(2), slab_s(2), slab_s(2)],
        compiler_params=_params(("parallel", "parallel")),
        name="inproj",
    )(h, g, win, qn, wuq, kvn, wukv, *tabs)


def _band_masks():
    length = SEQ // DIL
    neg = lambda ok: jnp.where(ok, 0.0, NEG_INF).astype(F32)
    d1 = (jnp.arange(BAND_TK)[None, :] - jnp.arange(BAND_TQ)[:, None])[None] \
        - RADIUS * jnp.arange(3)[:, None, None]
    d = jnp.arange(length)[None, :] - jnp.arange(length)[:, None]
    step = 16 // DIL
    return (neg(jnp.abs(d1) <= RADIUS), neg(jnp.abs(d) <= RADIUS),
            neg((d % step == 0) & (jnp.abs(d) <= RADIUS * step)))


def _softmax_pv(z, v):
    m = jnp.max(z, axis=-1, keepdims=True)
    p = jnp.exp(z - m)
    den = jnp.sum(p, axis=-1, keepdims=True)
    return _dot(p.astype(BF16), v) / den, m + jnp.log(den)


def _band_kernel(q_ref, k_ref, v_ref, m1_ref, m4_ref, m16_ref, o_ref,
                 xf_ref, qr_ref, kr_ref, vr_ref, os_ref, ls_ref):
    length = SEQ // DIL
    lane = lax.broadcasted_iota(jnp.int32, (BAND_TQ, LANES), 1)
    head0 = lane < HEAD_DIM

    for src, dst in ((q_ref, qr_ref), (k_ref, kr_ref), (v_ref, vr_ref)):
        for s in range(2):
            xf_ref[...] = src[:, s * LANES:(s + 1) * LANES].astype(F32)
            for r in range(DIL):
                dst[r, s] = xf_ref[pl.ds(r, length, stride=DIL), :].astype(BF16)

    def head_pair(q, k, v, masks):
        scores = []
        for t in range(2):
            qm = jnp.where(head0 if t == 0 else ~head0, q, jnp.zeros_like(q))
            scores.append(_dot_nt(qm, k))
        res = []
        for mask in masks:
            r0, r1 = _softmax_pv(scores[0] + mask, v), _softmax_pv(scores[1] + mask, v)
            res.append((jnp.where(head0, r0[0], r1[0]), jnp.where(head0, r0[1], r1[1])))
        return res

    def dil1_block(i, carry):
        q0 = pl.multiple_of(i * BAND_TQ, BAND_TQ)
        ws = pl.multiple_of(jnp.clip(q0 - RADIUS, 0, SEQ - BAND_TK), RADIUS)
        mask = m1_ref[(q0 - ws) // RADIUS]
        for s in range(2):
            lo = s * LANES
            (o, lse), = head_pair(q_ref[pl.ds(q0, BAND_TQ), lo:lo + LANES],
                                  k_ref[pl.ds(ws, BAND_TK), lo:lo + LANES],
                                  v_ref[pl.ds(ws, BAND_TK), lo:lo + LANES], (mask,))
            os_ref[0, s, pl.ds(q0, BAND_TQ), :] = o
            ls_ref[0, s, pl.ds(q0, BAND_TQ), :] = lse
        return carry

    lax.fori_loop(0, SEQ // BAND_TQ, dil1_block, 0, unroll=2)

    def residue_block(r, carry):
        rows = pl.ds(r, length, stride=DIL)
        for s in range(2):
            res = head_pair(qr_ref[r, s], kr_ref[r, s], vr_ref[r, s], (m4_ref[...], m16_ref[...]))
            for pat in range(2):
                os_ref[1 + pat, s, rows, :] = res[pat][0]
                ls_ref[1 + pat, s, rows, :] = res[pat][1]
        return carry

    lax.fori_loop(0, DIL, residue_block, 0, unroll=2)

    def mix_block(i, carry):
        rows = pl.ds(pl.multiple_of(i * MIX_ROWS, MIX_ROWS), MIX_ROWS)
        for s in range(2):
            l1, l4, l16 = ls_ref[0, s, rows, :], ls_ref[1, s, rows, :], ls_ref[2, s, rows, :]
            m = jnp.maximum(jnp.maximum(l1, l4), l16)
            e1, e4, e16 = jnp.exp(l1 - m), jnp.exp(l4 - m), jnp.exp(l16 - m)
            den = e1 + e4 + e16
            o_ref[rows, s * LANES:(s + 1) * LANES] = (
                (e1 / den) * os_ref[0, s, rows, :] + (e4 / den) * os_ref[1, s, rows, :]
                + (e16 / den) * os_ref[2, s, rows, :]).astype(BF16)
        return carry

    lax.fori_loop(0, SEQ // MIX_ROWS, mix_block, 0)


def _band_attention(q, k, v, masks):
    b = q.shape[0]
    length = SEQ // DIL
    spec = pl.BlockSpec((None, SEQ, 256), lambda i: (i, 0, 0))
    return pl.pallas_call(
        _band_kernel,
        grid=(b,),
        in_specs=[spec, spec, spec, _const_spec((3, BAND_TQ, BAND_TK)),
                  _const_spec((length, length)), _const_spec((length, length))],
        out_specs=spec,
        out_shape=jax.ShapeDtypeStruct((b, SEQ, 256), BF16),
        scratch_shapes=[pltpu.VMEM((SEQ, LANES), F32),
                        pltpu.VMEM((DIL, 2, length, LANES), BF16),
                        pltpu.VMEM((DIL, 2, length, LANES), BF16),
                        pltpu.VMEM((DIL, 2, length, LANES), BF16),
                        pltpu.VMEM((3, 2, SEQ, LANES), F32),
                        pltpu.VMEM((3, 2, SEQ, LANES), F32)],
        compiler_params=_params(("parallel",)),
        name="band_mix",
    )(q, k, v, *masks)


def _pipelined_blocks(nblk, scores, consume, s0_ref, s1_ref):
    scores(0, s0_ref)

    def body(j, carry):
        b0 = 2 * j
        scores(b0 + 1, s1_ref)
        consume(b0, s0_ref)
        scores(b0 + 2, s0_ref)
        consume(b0 + 1, s1_ref)
        return carry

    lax.fori_loop(0, nblk // 2 - 1, body, 0)
    scores(nblk - 1, s1_ref)
    consume(nblk - 2, s0_ref)
    consume(nblk - 1, s1_ref)


def _block_rows(blk):
    start = blk * ATT_TQ
    if not isinstance(start, int):
        start = pl.multiple_of(start, ATT_TQ)
    return pl.ds(start, ATT_TQ)


def _mla_kernel(q_ref, k_ref, v_ref, o_ref, s0_ref, s1_ref):
    tq = ATT_TQ

    def head_pair(s, carry):
        vt = v_ref[s].astype(F32).T.astype(BF16)
        sub0 = lax.broadcasted_iota(jnp.int32, (LANES, tq), 0) < HEAD_DIM

        def scores(blk, s_ref):
            rows = _block_rows(blk)
            for t in range(2):
                s_ref[t] = _dot_nt(k_ref[2 * s + t], q_ref[2 * s + t, rows, :])

        def consume(blk, s_ref):
            rows = _block_rows(blk)
            outs = []
            for t in range(2):
                sc = s_ref[t]
                p = jnp.exp2(sc - jnp.max(sc, axis=0, keepdims=True))
                den = jnp.sum(p, axis=0, keepdims=True)
                outs.append(_dot(vt, p.astype(BF16)) / den)
            o_ref[s, rows, :] = jnp.where(sub0, outs[0], outs[1]).T.astype(BF16)

        _pipelined_blocks(SEQ // tq, scores, consume, s0_ref, s1_ref)
        return carry

    lax.fori_loop(0, 2, head_pair, 0)


def _mla_attention(q, k, v):
    b = q.shape[0]
    return pl.pallas_call(
        _mla_kernel,
        grid=(b,),
        in_specs=[pl.BlockSpec((None, 4, SEQ, LANES), lambda i: (i, 0, 0, 0)),
                  pl.BlockSpec((None, 4, SEQ, LANES), lambda i: (i, 0, 0, 0)),
                  pl.BlockSpec((None, 2, SEQ, LANES), lambda i: (i, 0, 0, 0))],
        out_specs=pl.BlockSpec((None, 2, SEQ, LANES), lambda i: (i, 0, 0, 0)),
        out_shape=jax.ShapeDtypeStruct((b, 2, SEQ, LANES), BF16),
        scratch_shapes=[pltpu.VMEM((2, SEQ, ATT_TQ), F32), pltpu.VMEM((2, SEQ, ATT_TQ), F32)],
        compiler_params=_params(("parallel",)),
        name="mla_attn",
    )(q, k, v)


def _diff_kernel(lam_ref, sub_ref, q_ref, k_ref, v_ref, o_ref, s0_ref, s1_ref, *, lam_init):
    tq = ATT_TQ
    lv = lam_ref[...]
    lam = (jnp.exp(jnp.sum(lv[0:1] * lv[1:2], axis=-1, keepdims=True))
           - jnp.exp(jnp.sum(lv[2:3] * lv[3:4], axis=-1, keepdims=True)) + lam_init)
    lane = lax.broadcasted_iota(jnp.int32, (tq, LANES), 1)
    head0 = lane < HEAD_DIM

    def head_pair(s, carry):
        vt = v_ref[s].astype(F32).T.astype(BF16)
        sub0 = lax.broadcasted_iota(jnp.int32, (LANES, tq), 0) < HEAD_DIM

        def scores(blk, s_ref):
            q = q_ref[s, _block_rows(blk), :]
            for u in range(4):
                lo = u * DIFF_QK_DIM
                qm = jnp.where((lane >= lo) & (lane < lo + DIFF_QK_DIM), q, jnp.zeros_like(q))
                s_ref[u] = _dot_nt(k_ref[s], qm)

        def consume(blk, s_ref):
            rows = _block_rows(blk)
            outs = []
            for t in range(2):
                s1, s2 = s_ref[2 * t], s_ref[2 * t + 1]
                p1 = jnp.exp2(s1 - jnp.max(s1, axis=0, keepdims=True))
                p2 = jnp.exp2(s2 - jnp.max(s2, axis=0, keepdims=True))
                d1 = jnp.sum(p1, axis=0, keepdims=True)
                d2 = jnp.sum(p2, axis=0, keepdims=True)
                pn = p1 - (lam * d1 / d2) * p2
                outs.append(_dot(vt, pn.astype(BF16)) / d1)
            o = jnp.where(sub0, outs[0], outs[1]).T
            o2 = o * o
            ss0 = jnp.sum(jnp.where(head0, o2, 0.0), axis=-1, keepdims=True)
            ss1 = jnp.sum(jnp.where(head0, 0.0, o2), axis=-1, keepdims=True)
            ms = jnp.where(head0, ss0, ss1) * (1.0 / HEAD_DIM)
            y = o * lax.rsqrt(ms + DIFF_EPS) * sub_ref[...]
            o_ref[s, rows, :] = (y * (1.0 - lam_init)).astype(BF16)

        _pipelined_blocks(SEQ // tq, scores, consume, s0_ref, s1_ref)
        return carry

    lax.fori_loop(0, 2, head_pair, 0)


def _diff_attention(lam_vecs, subln2, q, k, v, lam_init):
    b = q.shape[0]
    spec = pl.BlockSpec((None, 2, SEQ, LANES), lambda i: (i, 0, 0, 0))
    return pl.pallas_call(
        functools.partial(_diff_kernel, lam_init=lam_init),
        grid=(b,),
        in_specs=[pl.BlockSpec((4, DIFF_QK_DIM), lambda i: (0, 0)),
                  pl.BlockSpec((1, LANES), lambda i: (0, 0)),
                  spec, spec, spec],
        out_specs=spec,
        out_shape=jax.ShapeDtypeStruct((b, 2, SEQ, LANES), BF16),
        scratch_shapes=[pltpu.VMEM((4, SEQ, ATT_TQ), F32), pltpu.VMEM((4, SEQ, ATT_TQ), F32)],
        compiler_params=_params(("parallel",)),
        name="diff_attn",
    )(lam_vecs, subln2, q, k, v)


def _na_bias_kernel(rpb_ref, o_ref):
    nr = 2 * NA_WIN_ROWS - 1
    nc = 2 * NA_WIN_COLS - 1
    qc = lax.broadcasted_iota(jnp.int32, (GRID_W, LANES), 0)
    lane = lax.broadcasted_iota(jnp.int32, (GRID_W, LANES), 1)
    kc = lane % GRID_W
    left = lane < GRID_W
    start = jnp.clip(qc - NA_WIN_COLS // 2, 0, GRID_W - NA_WIN_COLS)
    col_ok = (kc >= start) & (kc < start + NA_WIN_COLS)
    idx_c = jnp.clip(kc - qc, -(NA_WIN_COLS - 1), NA_WIN_COLS - 1) + (NA_WIN_COLS - 1)
    sel = [idx_c == t for t in range(nc)]

    def body(i, carry):
        h = i // NA_TABS
        d = i % NA_TABS - NA_WIN_ROWS
        acc = jnp.zeros((GRID_W, LANES), F32)
        r_lo = jnp.clip(d + NA_WIN_ROWS - 1, 0, nr - 1)
        r_hi = jnp.clip(d + NA_WIN_ROWS, 0, nr - 1)
        for t in range(nc):
            v_lo = rpb_ref[(h * nr + r_lo) * nc + t]
            v_hi = rpb_ref[(h * nr + r_hi) * nc + t]
            acc = acc + jnp.where(sel[t], jnp.where(left, v_lo, v_hi), 0.0)
        dr = jnp.where(left, d, d + 1)
        ok = col_ok & (dr >= -(NA_WIN_ROWS - 1)) & (dr <= NA_WIN_ROWS - 1)
        o_ref[i] = jnp.where(ok, acc, NEG_INF)
        return carry

    lax.fori_loop(0, GROUP_HEADS * NA_TABS, body, 0)


def _na_bias(rpb):
    return pl.pallas_call(
        _na_bias_kernel,
        in_specs=[pl.BlockSpec(memory_space=pltpu.SMEM)],
        out_specs=pl.BlockSpec(memory_space=pltpu.VMEM),
        out_shape=jax.ShapeDtypeStruct((GROUP_HEADS * NA_TABS, GRID_W, LANES), F32),
        name="na_bias",
    )(rpb.reshape(-1))


def _na_kernel(bias_ref, q_ref, k_ref, v_ref, o_ref):
    rows = SEQ // GRID_W
    tq = NA_ROWS * GRID_W
    tk = NA_KROWS * GRID_W
    lane = lax.broadcasted_iota(jnp.int32, (tq, LANES), 1)
    head0 = lane < HEAD_DIM
    left = lax.broadcasted_iota(jnp.int32, (GRID_W, LANES), 1) < GRID_W
    neg = jnp.full((GRID_W, LANES), NEG_INF, F32)

    def head_pair(pair, carry):
        for blk in range(rows // NA_ROWS):
            r0 = blk * NA_ROWS
            ws = min(max(r0 - NA_WIN_ROWS // 2, 0), rows - NA_KROWS)
            q = q_ref[pair, r0 * GRID_W:r0 * GRID_W + tq, :]
            kw = k_ref[pair, ws * GRID_W:ws * GRID_W + tk, :]
            vw = v_ref[pair, ws * GRID_W:ws * GRID_W + tk, :]
            outs = []
            for t in range(2):
                tab0 = (2 * pair + t) * NA_TABS
                rows_bias = []
                for ri in range(NA_ROWS):
                    r = r0 + ri
                    lo_r = min(max(r - NA_WIN_ROWS // 2, 0), rows - NA_WIN_ROWS) - r
                    pieces = []
                    for j2 in range(NA_KROWS // 2):
                        d = ws + 2 * j2 - r
                        ok_l = lo_r <= d <= lo_r + NA_WIN_ROWS - 1
                        ok_r = lo_r <= d + 1 <= lo_r + NA_WIN_ROWS - 1
                        if not (ok_l or ok_r):
                            pieces.append(neg)
                            continue
                        tab = bias_ref[tab0 + d + NA_WIN_ROWS]
                        if not ok_l:
                            tab = jnp.where(left, NEG_INF, tab)
                        elif not ok_r:
                            tab = jnp.where(left, tab, NEG_INF)
                        pieces.append(tab)
                    rows_bias.append(jnp.concatenate(pieces, axis=1))
                bias = jnp.concatenate(rows_bias, axis=0)
                qm = jnp.where(head0 if t == 0 else ~head0, q, jnp.zeros_like(q))
                sc = _dot_nt(qm, kw) + bias
                m = jnp.max(sc, axis=-1, keepdims=True)
                p = jnp.exp(sc - m)
                den = jnp.sum(p, axis=-1, keepdims=True)
                outs.append(_dot(p.astype(BF16), vw) / den)
            o_ref[pair, r0 * GRID_W:r0 * GRID_W + tq, :] = (
                jnp.where(head0, outs[0], outs[1]).astype(BF16))
        return carry

    lax.fori_loop(0, 2, head_pair, 0)


def _na_attention(bias, q, k, v):
    b = q.shape[0]
    spec = pl.BlockSpec((None, 2, SEQ, LANES), lambda i: (i, 0, 0, 0))
    return pl.pallas_call(
        _na_kernel,
        grid=(b,),
        in_specs=[_const_spec((GROUP_HEADS * NA_TABS, GRID_W, LANES)), spec, spec, spec],
        out_specs=spec,
        out_shape=jax.ShapeDtypeStruct((b, 2, SEQ, LANES), BF16),
        compiler_params=_params(("parallel",)),
        name="na_attn",
    )(bias, q, k, v)


def _oproj_kernel(h_ref, oa_ref, ob_ref, oc_ref, od_ref, wo_ref, g_ref,
                  h1_ref, hnp_ref, mix_ref, hs_ref):
    mix_ref[:, 0:256] = oa_ref[...]
    mix_ref[:, 256:384] = ob_ref[0]
    mix_ref[:, 384:512] = ob_ref[1]
    mix_ref[:, 512:640] = oc_ref[0]
    mix_ref[:, 640:768] = oc_ref[1]
    mix_ref[:, 768:896] = od_ref[0]
    mix_ref[:, 896:1024] = od_ref[1]
    h1 = h_ref[...] + _dot(mix_ref[...], wo_ref[...])
    h1_ref[...] = h1
    hn = _rms(h1, g_ref[...], NORM_EPS)
    for cb in range(D_MODEL // LANES):
        lo = cb * LANES
        hs_ref[cb] = hn[:, lo:lo + LANES]
        for g in range(FF_GROUPS):
            hnp_ref[g, :, lo:lo + LANES] = hs_ref[cb, pl.ds(g, FF_GP, stride=FF_GROUPS), :].astype(BF16)


def _oproj(h, oa, ob, oc, od, wo, g):
    b = h.shape[0]
    tm = FF_TS
    nt = SEQ // tm
    row = lambda w: pl.BlockSpec((None, tm, w), lambda i, j: (i, j, 0))
    slab = pl.BlockSpec((None, 2, tm, LANES), lambda i, j: (i, 0, j, 0))
    return pl.pallas_call(
        _oproj_kernel,
        grid=(b, nt),
        in_specs=[row(D_MODEL), row(256), slab, slab, slab,
                  _const_spec((D_MODEL, D_MODEL)), _const_spec((1, D_MODEL))],
        out_specs=[row(D_MODEL),
                   pl.BlockSpec((None, None, FF_GROUPS, FF_GP, D_MODEL), lambda i, j: (i, j, 0, 0, 0))],
        out_shape=[jax.ShapeDtypeStruct((b, SEQ, D_MODEL), F32),
                   jax.ShapeDtypeStruct((b, nt, FF_GROUPS, FF_GP, D_MODEL), BF16)],
        scratch_shapes=[pltpu.VMEM((tm, D_MODEL), BF16), pltpu.VMEM((D_MODEL // LANES, tm, LANES), F32)],
        compiler_params=_params(("parallel", "parallel")),
        name="oproj",
    )(h, oa, ob, oc, od, wo, g)


def _gelu_tanh(x):
    c = math.sqrt(2.0 / math.pi)
    return x * (0.5 * (1.0 + jnp.tanh(c * (x + 0.044715 * (x * x * x)))))


_FF_OFF = {7: FF_HALO, 0: FF_HALO + 7 * FF_GP}
_FF_OFF.update({g: FF_HALO + g * FF_GP for g in range(1, 7)})
_FF_ROWS = FF_TS + 2 * FF_HALO


def _ffn_kernel(hnp_ref, prev_ref, next_ref, h1_ref, p_ref, wup_ref, cw_ref, cb_ref, wdn_ref,
                pg_ref, wpg_ref, wpp_ref, fg_ref, o_ref, ext_ref, ug_ref, uv_ref, a_ref, nat_ref,
                *, last_layer):
    t = pl.program_id(1)
    gp = FF_GP
    prev = prev_ref[...]
    nxt = next_ref[...]
    ext_ref[0:FF_HALO, :] = jnp.where(t > 0, prev, jnp.zeros_like(prev))
    ext_ref[_FF_ROWS - FF_HALO:, :] = jnp.where(t < pl.num_programs(1) - 1, nxt, jnp.zeros_like(nxt))
    for g in range(FF_GROUPS):
        ext_ref[_FF_OFF[g]:_FF_OFF[g] + gp, :] = hnp_ref[g]
    ext = ext_ref[...]

    def window(u_ref, slot, g, shift):
        if shift < 0 and g == 0:
            lo = _FF_OFF[7] - 1
        elif shift > 0 and g == FF_GROUPS - 1:
            lo = _FF_OFF[0] + 1
        else:
            lo = _FF_OFF[g + shift]
        return u_ref[slot, lo:lo + gp, :]

    def conv(u_ref, slot, g, col):
        w = cw_ref[:, col:col + FF_CHUNK]
        return (window(u_ref, slot, g, -1) * w[0:1] + window(u_ref, slot, g, 0) * w[1:2]
                + window(u_ref, slot, g, 1) * w[2:3] + cb_ref[:, col:col + FF_CHUNK])

    for c in range(D_FF // FF_CHUNK):
        gcol = c * FF_CHUNK
        vcol = D_FF + c * FF_CHUNK
        slot = c % 2
        ug_ref[slot] = _dot(ext, wup_ref[:, gcol:gcol + FF_CHUNK])
        uv_ref[slot] = _dot(ext, wup_ref[:, vcol:vcol + FF_CHUNK])
        for g in range(FF_GROUPS):
            a_ref[g * gp:(g + 1) * gp, gcol:gcol + FF_CHUNK] = (
                _gelu_tanh(conv(ug_ref, slot, g, gcol)) * conv(uv_ref, slot, g, vcol)).astype(BF16)

    acc = _dot(a_ref[...], wdn_ref[...])
    for cb in range(D_MODEL // LANES):
        lo = cb * LANES
        for g in range(FF_GROUPS):
            nat_ref[cb, pl.ds(g, gp, stride=FF_GROUPS), :] = acc[g * gp:(g + 1) * gp, lo:lo + LANES]
        o_ref[:, lo:lo + LANES] = h1_ref[:, lo:lo + LANES] + nat_ref[cb]
    h2 = o_ref[...]
    z = _dot(_rms(h2, pg_ref[...], NORM_EPS).astype(BF16), wpg_ref[...])
    gate = 1.0 / (1.0 + jnp.exp(-z))
    e = _dot(p_ref[...].astype(BF16), wpp_ref[...])
    out = h2 + gate * e
    o_ref[...] = _rms(out, fg_ref[...], NORM_EPS) if last_layer else out


def _ffn(hnp, h1, p, wup, cw, cb, wdn, pg, wpg, wpp, final_gain, last_layer):
    b = h1.shape[0]
    ts = FF_TS
    nt = SEQ // ts
    gblk = FF_GP // FF_HALO
    row = lambda w: pl.BlockSpec((None, ts, w), lambda i, j: (i, j, 0))
    full = _const_spec
    halo = lambda imap: pl.BlockSpec((None, None, None, FF_HALO, D_MODEL), imap)
    return pl.pallas_call(
        functools.partial(_ffn_kernel, last_layer=last_layer),
        grid=(b, nt),
        in_specs=[pl.BlockSpec((None, None, FF_GROUPS, FF_GP, D_MODEL), lambda i, j: (i, j, 0, 0, 0)),
                  halo(lambda i, j: (i, jnp.maximum(j - 1, 0), FF_GROUPS - 1, gblk - 1, 0)),
                  halo(lambda i, j: (i, jnp.minimum(j + 1, nt - 1), 0, 0, 0)),
                  row(D_MODEL), row(PLE_DIM),
                  full((D_MODEL, 2 * D_FF)), full((3, 2 * D_FF)), full((1, 2 * D_FF)),
                  full((D_FF, D_MODEL)), full((1, D_MODEL)), full((D_MODEL, D_MODEL)),
                  full((PLE_DIM, D_MODEL)), full((1, D_MODEL))],
        out_specs=row(D_MODEL),
        out_shape=jax.ShapeDtypeStruct((b, SEQ, D_MODEL), F32),
        scratch_shapes=[pltpu.VMEM((_FF_ROWS, D_MODEL), BF16),
                        pltpu.VMEM((2, _FF_ROWS, FF_CHUNK), F32),
                        pltpu.VMEM((2, _FF_ROWS, FF_CHUNK), F32),
                        pltpu.VMEM((ts, D_FF), BF16),
                        pltpu.VMEM((D_MODEL // LANES, ts, LANES), F32)],
        compiler_params=_params(("parallel", "parallel")),
        name="ffn_ple",
    )(hnp, hnp, hnp, h1, p, wup, cw, cb, wdn, pg, wpg, wpp, final_gain)


def _prep_win(w):
    a, bq, bkv, bkr, c, d = jnp.split(w, [768, 1152, 1408, 1440, 2208], axis=1)
    scale = HEAD_DIM ** -0.5
    a = jnp.concatenate([a[:, :256] * scale, a[:, 256:]], axis=1)
    d = jnp.concatenate([d[:, :256] * scale, d[:, 256:]], axis=1)
    z = lambda n: jnp.zeros((w.shape[0], n), w.dtype)
    kr = jnp.concatenate([z(MLA_NOPE_DIM), bkr, z(LANES - MLA_NOPE_DIM - MLA_ROPE_DIM)], axis=1)
    return jnp.concatenate([a, bq, bkv, kr, c, d], axis=1).astype(BF16)


def _prep_wuq(w):
    w = w.reshape(MLA_Q_RANK, 4, MLA_NOPE_DIM + MLA_ROPE_DIM)
    w = jnp.pad(w, ((0, 0), (0, 0), (0, LANES - MLA_NOPE_DIM - MLA_ROPE_DIM)))
    return w.reshape(MLA_Q_RANK, 4 * LANES).astype(BF16)


def _prep_wukv(w):
    w = w.reshape(MLA_KV_RANK, 4, MLA_NOPE_DIM + HEAD_DIM)
    kn = jnp.pad(w[:, :, :MLA_NOPE_DIM], ((0, 0), (0, 0), (0, LANES - MLA_NOPE_DIM)))
    v = w[:, :, MLA_NOPE_DIM:]
    return jnp.concatenate([kn.reshape(MLA_KV_RANK, 4 * LANES),
                            v.reshape(MLA_KV_RANK, 4 * HEAD_DIM)], axis=1).astype(BF16)


def kernel(x, p, attn_norm, w_in, mla_q_norm, w_uq, mla_kv_norm, w_ukv, lam_q1, lam_k1, lam_q2, lam_k2,
           diff_subln, na_rpb, w_o, ffn_norm, w_up, conv_w, conv_b, w_down, ple_norm, w_ple_gate,
           w_ple_proj, final_norm):
    mla_scale = (MLA_NOPE_DIM + MLA_ROPE_DIM) ** -0.5 * LOG2E
    diff_scale = DIFF_QK_DIM ** -0.5 * LOG2E
    tabs = (_rope_tables(HEAD_DIM, (0, 64), 1.0),
            _rope_tables(MLA_ROPE_DIM, (64,), mla_scale),
            _rope_tables(MLA_ROPE_DIM, (64,), 1.0),
            _rope_tables(DIFF_QK_DIM, (0, 32, 64, 96), diff_scale),
            _rope_tables(DIFF_QK_DIM, (0, 32, 64, 96), 1.0))
    masks = _band_masks()
    row = lambda v: v.reshape(1, -1)
    h = x
    for i in range(DEPTH):
        aq, ak, av, bq, bk, bv, cq, ck, cv, dq, dk, dv = _inproj(
            h, row(attn_norm[i]), _prep_win(w_in[i]), row(mla_q_norm[i]), _prep_wuq(w_uq[i]),
            row(mla_kv_norm[i]), _prep_wukv(w_ukv[i]), tabs)
        oa = _band_attention(aq, ak, av, masks)
        ob = _mla_attention(bq, bk, bv)
        lam_init = 0.8 - 0.6 * math.exp(-0.3 * i)
        lam_vecs = jnp.stack([lam_q1[i], lam_k1[i], lam_q2[i], lam_k2[i]])
        oc = _diff_attention(lam_vecs, row(jnp.tile(diff_subln[i], 2)), cq, ck, cv, lam_init)
        od = _na_attention(_na_bias(na_rpb[i]), dq, dk, dv)
        h1, hnp = _oproj(h, oa, ob, oc, od, w_o[i].astype(BF16), row(ffn_norm[i]))
        h = _ffn(hnp, h1, p[i], w_up[i].astype(BF16), conv_w[i], row(conv_b[i]),
                 w_down[i].astype(BF16), row(ple_norm[i]), w_ple_gate[i].astype(BF16),
                 w_ple_proj[i].astype(BF16), row(final_norm), last_layer=i == DEPTH - 1)
    return h
```
